```python
import jax
import jax.numpy as jnp
from jax import lax
import numpy as np

D_MODEL = 4096
BATCH = 1
SEQ = 8192
DEPTH = 4
DEC_BATCH = 8
DEC_SEQ = 64
PAST_LEN = 4096

CHUNK = 64
HEAD_DIM_A = 64
A_WIDTH = D_MODEL // 4
A_HEADS = A_WIDTH // HEAD_DIM_A
A_KV_HEADS = 4
A_GROUP = A_HEADS // A_KV_HEADS
A_KV_WIDTH = A_KV_HEADS * HEAD_DIM_A
WINDOW = 128
WIN_CHUNKS = WINDOW // CHUNK
ROT_DIM_A = HEAD_DIM_A // 4
ROPE_THETA_A = 500000.0
B_HEAD_DIM = 256
B_WIDTH = D_MODEL // 2
B_HEADS = B_WIDTH // B_HEAD_DIM
RET_THETA = 10000.0
C_WIDTH = D_MODEL // 4
C_GROUPS = 4
C_GROUP_DIM = C_WIDTH // C_GROUPS
CMLP_CHUNK = 128
MIX_WIDTH = A_WIDTH + B_WIDTH + C_WIDTH
IN_SIZES = (A_WIDTH, A_KV_WIDTH, A_KV_WIDTH, B_WIDTH, B_WIDTH, B_WIDTH, B_WIDTH, C_WIDTH, C_WIDTH)
IN_COLS = A_WIDTH + 2 * A_KV_WIDTH + 4 * B_WIDTH + 2 * C_WIDTH
N_EXPERTS = 32
TOP_K = 4
EXPERT_DIM = D_MODEL // 4
SWIGLU_LIMIT = 7.0
SWIGLU_ALPHA = 1.702
PLE_DIM = 256
DN_ALPHA = (2 * DEPTH) ** 0.25
DN_BETA = (8 * DEPTH) ** -0.25
LN_EPS = 1e-5
RMS_EPS = 1e-6

kernel_name = "hybrid_stream_encoder_step"


def layer_norm(x, g, b):
    xf = x.astype(jnp.float32)
    mu = jnp.mean(xf, -1, keepdims=True)
    var = jnp.mean(jnp.square(xf - mu), -1, keepdims=True)
    return ((xf - mu) * lax.rsqrt(var + LN_EPS) * g + b).astype(x.dtype)


def rotary(x, pos, rot_dim, theta):
    half = rot_dim // 2
    inv = jnp.float32(theta) ** (-jnp.arange(half, dtype=jnp.float32) / half)
    ang = pos.astype(jnp.float32)[:, None] * inv[None, :]
    cos = jnp.cos(ang)[:, None, :]
    sin = jnp.sin(ang)[:, None, :]
    xr = x[..., :rot_dim].astype(jnp.float32)
    x1, x2 = xr[..., :half], xr[..., half:]
    rot = jnp.concatenate([x1 * cos - x2 * sin, x2 * cos + x1 * sin], -1).astype(x.dtype)
    return jnp.concatenate([rot, x[..., rot_dim:]], -1)


def mixer_inputs(x, pos, w_in, b_in, c_ln_g, c_ln_b):
    B, S = x.shape[:2]
    h = jnp.einsum('bsd,dc->bsc', x, w_in) + b_in
    parts, off = [], 0
    for n in IN_SIZES:
        parts.append(h[..., off:off + n])
        off += n
    aq, ak, av, bq, bk, bv, bg, cu, cv = parts
    aq = rotary(aq.reshape(B, S, A_HEADS, HEAD_DIM_A), pos, ROT_DIM_A, ROPE_THETA_A)
    ak = rotary(ak.reshape(B, S, A_KV_HEADS, HEAD_DIM_A), pos, ROT_DIM_A, ROPE_THETA_A)
    av = av.reshape(B, S, A_KV_HEADS, HEAD_DIM_A)
    bq = rotary(bq.reshape(B, S, B_HEADS, B_HEAD_DIM), pos, B_HEAD_DIM, RET_THETA)
    bk = rotary(bk.reshape(B, S, B_HEADS, B_HEAD_DIM), pos, B_HEAD_DIM, RET_THETA) * (B_HEAD_DIM ** -0.5)
    bv = bv.reshape(B, S, B_HEADS, B_HEAD_DIM)
    cu = jax.nn.gelu(cu)
    cv = layer_norm(jax.nn.gelu(cv), c_ln_g, c_ln_b)
    return aq, ak, av, bq, bk, bv, bg, cu, cv


def sink_softmax(s, sink, valid):
    sink = sink.astype(jnp.float32)
    s = jnp.where(valid, s, -jnp.inf)
    m = jnp.maximum(jnp.max(s, -1, keepdims=True), sink)
    e = jnp.exp(s - m)
    return e / (jnp.sum(e, -1, keepdims=True) + jnp.exp(sink - m))


def swa_prompt(q, k, v, sinks):
    B, S = q.shape[:2]
    nb = S // CHUNK
    W = WIN_CHUNKS
    qb = q.reshape(B, nb, CHUNK, A_KV_HEADS, A_GROUP, HEAD_DIM_A)

    def band(t):
        tp = jnp.pad(t, ((0, 0), (W * CHUNK, 0), (0, 0), (0, 0)))
        tp = tp.reshape(B, nb + W, CHUNK, A_KV_HEADS, HEAD_DIM_A)
        return jnp.concatenate([tp[:, o:o + nb] for o in range(W + 1)], axis=2)

    kb, vb = band(k), band(v)
    key_chunk = jnp.arange(nb)[:, None] + jnp.arange((W + 1) * CHUNK)[None, :] // CHUNK - W
    valid = (key_chunk >= 0)[None, :, None, None, None, :]
    s = jnp.einsum('bnqkgd,bnmkd->bnkgqm', qb, kb).astype(jnp.float32) * (HEAD_DIM_A ** -0.5)
    probs = sink_softmax(s, sinks.reshape(A_KV_HEADS, A_GROUP)[:, :, None, None], valid)
    o = jnp.einsum('bnkgqm,bnmkd->bnqkgd', probs.astype(vb.dtype), vb)
    return o.reshape(B, S, A_WIDTH)


def swa_sample(q, k_new, v_new, k_cache, v_cache, sinks):
    B, L = q.shape[:2]
    k_all = jnp.concatenate([k_cache.astype(k_new.dtype), k_new], 1)
    v_all = jnp.concatenate([v_cache.astype(v_new.dtype), v_new], 1)
    qg = q.reshape(B, L, A_KV_HEADS, A_GROUP, HEAD_DIM_A)
    s = jnp.einsum('bqkgd,bmkd->bkgqm', qg, k_all).astype(jnp.float32) * (HEAD_DIM_A ** -0.5)
    probs = sink_softmax(s, sinks.reshape(A_KV_HEADS, A_GROUP)[:, :, None, None], True)
    o = jnp.einsum('bkgqm,bmkd->bqkgd', probs.astype(v_all.dtype), v_all).reshape(B, L, A_WIDTH)
    n_keep = k_cache.shape[1]
    return o, k_all[:, -n_keep:], v_all[:, -n_keep:]


def retention_chunk(R, q, k, v, log_gamma):
    L = q.shape[2]
    idx = jnp.arange(L, dtype=jnp.float32)
    diff = idx[:, None] - idx[None, :]
    decay = jnp.where(diff >= 0, jnp.exp(log_gamma[:, None, None] * jnp.maximum(diff, 0.0)), 0.0)
    scores = jnp.einsum('bhid,bhjd->bhij', q, k) * decay
    inner = jnp.einsum('bhij,bhjv->bhiv', scores, v)
    cross = jnp.einsum('bhid,bhdv->bhiv', q, R) * jnp.exp(log_gamma[:, None] * (idx + 1.0))[:, :, None]
    k_dec = k * jnp.exp(log_gamma[:, None] * (L - 1.0 - idx))[:, :, None]
    R_new = R * jnp.exp(log_gamma * L)[:, None, None] + jnp.einsum('bhjd,bhjv->bhdv', k_dec, v)
    return inner + cross, R_new


def retention_prompt(q, k, v, log_gamma):
    B, S = q.shape[:2]
    nb = S // CHUNK

    def to_chunks(t):
        return t.reshape(B, nb, CHUNK, B_HEADS, -1).transpose(1, 0, 3, 2, 4).astype(jnp.float32)

    R0 = jnp.zeros((B, B_HEADS, B_HEAD_DIM, B_HEAD_DIM), jnp.float32)

    def step(R, xs):
        qc, kc, vc = xs
        o, R = retention_chunk(R, qc, kc, vc, log_gamma)
        return R, o

    R, o = lax.scan(step, R0, (to_chunks(q), to_chunks(k), to_chunks(v)))
    o = o.transpose(1, 0, 3, 2, 4).reshape(B, S, B_HEADS, B_HEAD_DIM)
    return o, R


def retention_sample(q, k, v, R, log_gamma):
    t = lambda a: a.transpose(0, 2, 1, 3).astype(jnp.float32)
    o, R_new = retention_chunk(R.astype(jnp.float32), t(q), t(k), t(v), log_gamma)
    return o.transpose(0, 2, 1, 3), R_new


def retention_gate(o, g):
    on = o * lax.rsqrt(jnp.mean(o * o, -1, keepdims=True) + RMS_EPS)
    return on.reshape(g.shape).astype(g.dtype) * jax.nn.silu(g)


def spatial_mask():
    i = jnp.arange(CMLP_CHUNK)
    return (i[None, :] // CHUNK) <= (i[:, None] // CHUNK)


def cmlp_prompt(u, v, ws, wb):
    B, S = u.shape[:2]
    nc = S // CMLP_CHUNK
    wm = jnp.where(spatial_mask(), ws, 0.0)
    vc = v.reshape(B, nc, CMLP_CHUNK, C_GROUPS, C_GROUP_DIM)
    sv = jnp.einsum('gij,bnjgc->bnigc', wm, vc) + wb.T[:, :, None]
    return (u.reshape(vc.shape) * sv).reshape(B, S, C_WIDTH)


def cmlp_sample(u, v, ws, wb):
    B, L = u.shape[:2]
    wm = jnp.where(spatial_mask(), ws, 0.0)[:, :L, :L]
    vg = v.reshape(B, L, C_GROUPS, C_GROUP_DIM)
    sv = jnp.einsum('gij,bjgc->bigc', wm, vg) + wb[:, :L].T[:, :, None]
    return (u.reshape(vg.shape) * sv).reshape(B, L, C_WIDTH)


def moe(x, router_w, router_b, exp_w_gu, exp_b_gu, exp_w_d, exp_b_d):
    shp = x.shape
    t = x.reshape(-1, shp[-1])
    logits = (t @ router_w + router_b).astype(jnp.float32)
    top_v, top_i = lax.top_k(logits, TOP_K)
    gates = jnp.einsum('tk,tke->te', jax.nn.softmax(top_v, -1),
                       jax.nn.one_hot(top_i, N_EXPERTS, dtype=jnp.float32)).astype(x.dtype)
    h = jnp.einsum('td,edf->tef', t, exp_w_gu) + exp_b_gu
    g = jnp.minimum(h[..., :EXPERT_DIM], SWIGLU_LIMIT)
    u = jnp.clip(h[..., EXPERT_DIM:], -SWIGLU_LIMIT, SWIGLU_LIMIT)
    a = (u + 1.0) * g * jax.nn.sigmoid(SWIGLU_ALPHA * g) * gates[..., None]
    y = jnp.einsum('tef,efd->td', a, exp_w_d) + gates @ exp_b_d
    return y.reshape(shp).astype(x.dtype)


def post_mixer(x, mix, p_l, w_out, b_out, ln1_g, ln1_b, router_w, router_b, exp_w_gu, exp_b_gu,
               exp_w_d, exp_b_d, ln2_g, ln2_b, ple_w, ple_gate_w):
    h = jnp.einsum('bsm,md->bsd', mix, w_out) + b_out
    x = layer_norm(DN_ALPHA * x + h, ln1_g, ln1_b)
    x = layer_norm(DN_ALPHA * x + moe(x, router_w, router_b, exp_w_gu, exp_b_gu, exp_w_d, exp_b_d), ln2_g, ln2_b)
    gate = jax.nn.sigmoid(jnp.einsum('bsd,de->bse', x, ple_gate_w))
    return x + gate * jnp.einsum('bsp,pd->bsd', p_l, ple_w)


def setup_inputs(seed: int = 0) -> dict:
    key = jax.random.key(seed)
    ks = jax.random.split(key, 28)
    f32 = jnp.float32
    nrm = lambda k, shape, scale=1.0: jax.random.normal(k, shape, f32) * scale
    win_rows = min(WINDOW, PAST_LEN)
    return {
        "x_prompt": nrm(ks[0], (BATCH, SEQ, D_MODEL)),
        "x_sample": nrm(ks[1], (DEC_BATCH, DEC_SEQ, D_MODEL)),
        "cache_a_k": nrm(ks[2], (DEPTH, DEC_BATCH, win_rows, A_KV_HEADS, HEAD_DIM_A)),
        "cache_a_v": nrm(ks[3], (DEPTH, DEC_BATCH, win_rows, A_KV_HEADS, HEAD_DIM_A)),
        "state_b": nrm(ks[4], (DEPTH, DEC_BATCH, B_HEADS, B_HEAD_DIM, B_HEAD_DIM)),
        "p_prompt": nrm(ks[5], (DEPTH, BATCH, SEQ, PLE_DIM)),
        "p_sample": nrm(ks[6], (DEPTH, DEC_BATCH, DEC_SEQ, PLE_DIM)),
        "w_in": nrm(ks[7], (DEPTH, D_MODEL, IN_COLS), D_MODEL ** -0.5),
        "b_in": nrm(ks[8], (DEPTH, IN_COLS), 0.01),
        "a_sinks": nrm(ks[9], (DEPTH, A_HEADS), 0.5),
        "c_ln_g": 1.0 + nrm(ks[10], (DEPTH, C_WIDTH), 0.05),
        "c_ln_b": nrm(ks[11], (DEPTH, C_WIDTH), 0.01),
        "c_ws": nrm(ks[12], (DEPTH, C_GROUPS, CMLP_CHUNK, CMLP_CHUNK), CMLP_CHUNK ** -0.5),
        "c_wb": 1.0 + nrm(ks[13], (DEPTH, C_GROUPS, CMLP_CHUNK), 0.1),
        "w_out": nrm(ks[14], (DEPTH, MIX_WIDTH, D_MODEL), DN_BETA * MIX_WIDTH ** -0.5),
        "b_out": nrm(ks[15], (DEPTH, D_MODEL), 0.01),
        "ln1_g": 1.0 + nrm(ks[16], (DEPTH, D_MODEL), 0.05),
        "ln1_b": nrm(ks[17], (DEPTH, D_MODEL), 0.01),
        "router_w": nrm(ks[18], (DEPTH, D_MODEL, N_EXPERTS), D_MODEL ** -0.5),
        "router_b": nrm(ks[19], (DEPTH, N_EXPERTS), 0.01),
        "exp_w_gu": nrm(ks[20], (DEPTH, N_EXPERTS, D_MODEL, 2 * EXPERT_DIM), D_MODEL ** -0.5),
        "exp_b_gu": nrm(ks[21], (DEPTH, N_EXPERTS, 2 * EXPERT_DIM), 0.01),
        "exp_w_d": nrm(ks[22], (DEPTH, N_EXPERTS, EXPERT_DIM, D_MODEL), DN_BETA * EXPERT_DIM ** -0.5),
        "exp_b_d": nrm(ks[23], (DEPTH, N_EXPERTS, D_MODEL), 0.01),
        "ln2_g": 1.0 + nrm(ks[24], (DEPTH, D_MODEL), 0.05),
        "ln2_b": nrm(ks[25], (DEPTH, D_MODEL), 0.01),
        "ple_w": nrm(ks[26], (DEPTH, PLE_DIM, D_MODEL), PLE_DIM ** -0.5),
        "ple_gate_w": nrm(ks[27], (DEPTH, D_MODEL, D_MODEL), D_MODEL ** -0.5),
    }


def reference(x_prompt, x_sample, cache_a_k, cache_a_v, state_b, p_prompt, p_sample,
              w_in, b_in, a_sinks, c_ln_g, c_ln_b, c_ws, c_wb, w_out, b_out, ln1_g, ln1_b,
              router_w, router_b, exp_w_gu, exp_b_gu, exp_w_d, exp_b_d, ln2_g, ln2_b,
              ple_w, ple_gate_w):
    log_gamma = jnp.log(1.0 - 2.0 ** (-5.0 - jnp.arange(B_HEADS, dtype=jnp.float32)))
    pos_p = jnp.arange(x_prompt.shape[1])
    pos_s = PAST_LEN + jnp.arange(x_sample.shape[1])
    xp, xs = x_prompt, x_sample
    ak_p, av_p, rb_p, ak_s, av_s, rb_s, cv_s = [], [], [], [], [], [], []
    for l in range(DEPTH):
        post = dict(w_out=w_out[l], b_out=b_out[l], ln1_g=ln1_g[l], ln1_b=ln1_b[l],
                    router_w=router_w[l], router_b=router_b[l], exp_w_gu=exp_w_gu[l],
                    exp_b_gu=exp_b_gu[l], exp_w_d=exp_w_d[l], exp_b_d=exp_b_d[l],
                    ln2_g=ln2_g[l], ln2_b=ln2_b[l], ple_w=ple_w[l], ple_gate_w=ple_gate_w[l])
        aq, ak, av, bq, bk, bv, bg, cu, cv = mixer_inputs(xp, pos_p, w_in[l], b_in[l], c_ln_g[l], c_ln_b[l])
        oa = swa_prompt(aq, ak, av, a_sinks[l])
        ob, R_p = retention_prompt(bq, bk, bv, log_gamma)
        oc = cmlp_prompt(cu, cv, c_ws[l], c_wb[l])
        mix = jnp.concatenate([oa, retention_gate(ob, bg), oc], -1)
        xp = post_mixer(xp, mix, p_prompt[l], **post)
        ak_p.append(ak[:, -WINDOW:])
        av_p.append(av[:, -WINDOW:])
        rb_p.append(R_p.astype(state_b.dtype))
        aq, ak, av, bq, bk, bv, bg, cu, cv = mixer_inputs(xs, pos_s, w_in[l], b_in[l], c_ln_g[l], c_ln_b[l])
        oa, k_win, v_win = swa_sample(aq, ak, av, cache_a_k[l], cache_a_v[l], a_sinks[l])
        ob, R_s = retention_sample(bq, bk, bv, state_b[l], log_gamma)
        oc = cmlp_sample(cu, cv, c_ws[l], c_wb[l])
        mix = jnp.concatenate([oa, retention_gate(ob, bg), oc], -1)
        xs = post_mixer(xs, mix, p_sample[l], **post)
        ak_s.append(k_win)
        av_s.append(v_win)
        rb_s.append(R_s.astype(state_b.dtype))
        cv_s.append(cv)
    return (xp, xs, jnp.stack(ak_p), jnp.stack(av_p), jnp.stack(rb_p),
            jnp.stack(ak_s), jnp.stack(av_s), jnp.stack(rb_s), jnp.stack(cv_s))
```

```python
import functools

import jax
import jax.numpy as jnp
from jax import lax
from jax.experimental import pallas as pl
from jax.experimental.pallas import tpu as pltpu

D_MODEL = 4096
CHUNK = 64
PAST_LEN = 4096
HEAD_DIM_A = 64
A_WIDTH = 1024
A_KV_HEADS = 4
A_GROUP = 4
A_KV_WIDTH = 256
ROT_DIM_A = 16
ROPE_THETA_A = 500000.0
B_HEAD_DIM = 256
B_WIDTH = 2048
B_HEADS = 8
RET_THETA = 10000.0
RET_BLOCK = 256
C_WIDTH = 1024
C_GROUPS = 4
C_GROUP_DIM = 256
CMLP_CHUNK = 128
OFF_AQ, OFF_AK, OFF_AV = 0, 1024, 1280
OFF_BQ, OFF_BK, OFF_BV, OFF_BG = 1536, 3584, 5632, 7680
OFF_CU, OFF_CV = 9728, 10752
IN_COLS = 11776
N_EXPERTS = 32
TOP_K = 4
EXPERT_DIM = 1024
SWIGLU_LIMIT = 7.0
SWIGLU_ALPHA = 1.702
PLE_DIM = 256
LN_EPS = 1e-5
RMS_EPS = 1e-6

LANES = 128
MOE_TILE = 256
VMEM_LIMIT = 56 * 1024 * 1024

BF16 = jnp.bfloat16
F32 = jnp.float32


def _params(*sem):
    return pltpu.CompilerParams(dimension_semantics=sem, vmem_limit_bytes=VMEM_LIMIT)


def _row_tile(n, target):
    best = None
    for t in range(16, min(n, target) + 1, 16):
        if n % t == 0:
            best = t
    assert best is not None, n
    return best


def _layer_norm(x, g, b):
    mu = jnp.mean(x, -1, keepdims=True)
    xc = x - mu
    var = jnp.mean(xc * xc, -1, keepdims=True)
    return xc * lax.rsqrt(var + LN_EPS) * g + b


def _gelu(x):
    c = 0.7978845608028654
    return 0.5 * x * (1.0 + jnp.tanh(c * (x + 0.044715 * (x * x * x))))


def _sigmoid(x):
    return 1.0 / (1.0 + jnp.exp(-x))


def _in_proj_kernel(x_ref, w_ref, b_ref, o_ref):
    w = w_ref[...].astype(BF16)
    o_ref[...] = jnp.dot(x_ref[...], w, preferred_element_type=F32) + b_ref[...]


def in_proj(xb, w_in, b_in3, l):
    T = xb.shape[0]
    n = w_in.shape[2]
    tm = _row_tile(T, 1088)
    tn = 512
    return pl.pallas_call(
        _in_proj_kernel,
        grid=(T // tm, n // tn),
        in_specs=[
            pl.BlockSpec((tm, D_MODEL), lambda i, j: (i, 0)),
            pl.BlockSpec((None, D_MODEL, tn), lambda i, j: (l, 0, j)),
            pl.BlockSpec((None, 1, tn), lambda i, j: (l, 0, j)),
        ],
        out_specs=pl.BlockSpec((tm, tn), lambda i, j: (i, j)),
        out_shape=jax.ShapeDtypeStruct((T, n), F32),
        compiler_params=_params("parallel", "parallel"),
        name="in_proj",
    )(xb, w_in, b_in3)


def _out_proj_kernel(oa_ref, ob_ref, oc_ref, w_ref, b_ref, o_ref):
    w = w_ref[...].astype(BF16)
    acc = jnp.dot(oa_ref[...], w[:A_WIDTH], preferred_element_type=F32)
    acc += jnp.dot(ob_ref[...], w[A_WIDTH:A_WIDTH + B_WIDTH], preferred_element_type=F32)
    acc += jnp.dot(oc_ref[...], w[A_WIDTH + B_WIDTH:], preferred_element_type=F32)
    o_ref[...] = acc + b_ref[...]


def out_proj(oa, ob, oc, w_out, b_out3, l):
    T = oa.shape[0]
    tm = _row_tile(T, 1088)
    tn = 512
    return pl.pallas_call(
        _out_proj_kernel,
        grid=(T // tm, D_MODEL // tn),
        in_specs=[
            pl.BlockSpec((tm, A_WIDTH), lambda i, j: (i, 0)),
            pl.BlockSpec((tm, B_WIDTH), lambda i, j: (i, 0)),
            pl.BlockSpec((tm, C_WIDTH), lambda i, j: (i, 0)),
            pl.BlockSpec((None, D_MODEL, tn), lambda i, j: (l, 0, j)),
            pl.BlockSpec((None, 1, tn), lambda i, j: (l, 0, j)),
        ],
        out_specs=pl.BlockSpec((tm, tn), lambda i, j: (i, j)),
        out_shape=jax.ShapeDtypeStruct((T, D_MODEL), F32),
        compiler_params=_params("parallel", "parallel"),
        name="out_proj",
    )(oa, ob, oc, w_out, b_out3)


def _rot_a(x, tab):
    c, s_lo, s_hi = tab[:, :LANES], tab[:, LANES:2 * LANES], tab[:, 2 * LANES:]
    half = ROT_DIM_A // 2
    outs = []
    for j in range(x.shape[1] // LANES):
        xc = x[:, j * LANES:(j + 1) * LANES]
        up = pltpu.roll(xc, LANES - half, 1)
        dn = pltpu.roll(xc, half, 1)
        outs.append(xc * c + up * s_lo + dn * s_hi)
    return outs[0] if len(outs) == 1 else jnp.concatenate(outs, axis=1)


def _attend(q, k, v, valid, sink_ref, l, o_ref, acc_ref):
    R = q.shape[0]
    qb = q.astype(BF16)
    kb = k.astype(BF16)
    vb = v.astype(BF16)
    grp = lax.broadcasted_iota(jnp.int32, (A_GROUP * R, 1), 0) // R
    for kh in range(A_KV_HEADS):
        base = kh * A_GROUP
        qs = jnp.concatenate(
            [qb[:, (base + g) * HEAD_DIM_A:(base + g + 1) * HEAD_DIM_A] for g in range(A_GROUP)], axis=0)
        k_h = kb[:, kh * HEAD_DIM_A:(kh + 1) * HEAD_DIM_A]
        v_h = vb[:, kh * HEAD_DIM_A:(kh + 1) * HEAD_DIM_A]
        s = lax.dot_general(qs, k_h, (((1,), (1,)), ((), ())), preferred_element_type=F32)
        if valid is not None:
            s = jnp.where(valid, s, -jnp.inf)
        sink = jnp.zeros((A_GROUP * R, 1), F32)
        for g in range(A_GROUP):
            sink = jnp.where(grp == g, sink_ref[l, base + g], sink)
        m = jnp.maximum(jnp.max(s, -1, keepdims=True), sink)
        e = jnp.exp(s - m)
        p = e / (jnp.sum(e, -1, keepdims=True) + jnp.exp(sink - m))
        o = jnp.dot(p.astype(BF16), v_h, preferred_element_type=F32)
        for g in range(A_GROUP):
            h = base + g
            acc_ref[:, h * HEAD_DIM_A:(h + 1) * HEAD_DIM_A] = o[g * R:(g + 1) * R]
    o_ref[...] = acc_ref[...].astype(o_ref.dtype)


def _attn_prompt_kernel(sink_ref, q_ref, kp_ref, kc_ref, vp_ref, vc_ref, tq_ref, tp_ref, o_ref, kr_ref, acc_ref,
                        *, l):
    i = pl.program_id(0)
    R = q_ref.shape[0]
    tq = tq_ref[...]
    q = _rot_a(q_ref[...], tq) * (HEAD_DIM_A ** -0.5)
    kc = _rot_a(kc_ref[...], tq)
    kp = _rot_a(kp_ref[...], tp_ref[...])
    kr_ref[...] = kc
    k = jnp.concatenate([kp, kc], axis=0)
    v = jnp.concatenate([vp_ref[...], vc_ref[...]], axis=0)
    n_q = A_GROUP * R
    qc = (lax.broadcasted_iota(jnp.int32, (n_q, 2 * R), 0) % R) // CHUNK
    col = lax.broadcasted_iota(jnp.int32, (n_q, 2 * R), 1)
    kc_idx = col // CHUNK - R // CHUNK
    lo = jnp.where(i == 0, 0, -(R // CHUNK))
    valid = (kc_idx <= qc) & (kc_idx >= jnp.maximum(qc - 2, lo))
    _attend(q, k, v, valid, sink_ref, l, o_ref, acc_ref)


def attn_prompt(h, tab_a, sinks, l, S):
    R = 2 * CHUNK
    nb = S // R
    prev = lambda i: jnp.maximum(i - 1, 0)
    return pl.pallas_call(
        functools.partial(_attn_prompt_kernel, l=l),
        grid=(nb,),
        in_specs=[
            pl.BlockSpec(memory_space=pltpu.SMEM),
            pl.BlockSpec((R, A_WIDTH), lambda i: (i, 0)),
            pl.BlockSpec((R, A_KV_WIDTH), lambda i: (prev(i), OFF_AK // A_KV_WIDTH)),
            pl.BlockSpec((R, A_KV_WIDTH), lambda i: (i, OFF_AK // A_KV_WIDTH)),
            pl.BlockSpec((R, A_KV_WIDTH), lambda i: (prev(i), OFF_AV // A_KV_WIDTH)),
            pl.BlockSpec((R, A_KV_WIDTH), lambda i: (i, OFF_AV // A_KV_WIDTH)),
            pl.BlockSpec((R, 3 * LANES), lambda i: (i, 0)),
            pl.BlockSpec((R, 3 * LANES), lambda i: (prev(i), 0)),
        ],
        out_specs=[
            pl.BlockSpec((R, A_WIDTH), lambda i: (i, 0)),
            pl.BlockSpec((R, A_KV_WIDTH), lambda i: (i, 0)),
        ],
        out_shape=[
            jax.ShapeDtypeStruct((S, A_WIDTH), BF16),
            jax.ShapeDtypeStruct((S, A_KV_WIDTH), F32),
        ],
        scratch_shapes=[pltpu.VMEM((R, A_WIDTH), F32)],
        compiler_params=_params("parallel"),
        name="attn_prompt",
    )(sinks, h, h, h, h, h, tab_a, tab_a)


def _attn_sample_kernel(sink_ref, q_ref, kn_ref, vn_ref, kcache_ref, vcache_ref, tq_ref, o_ref, kr_ref, acc_ref,
                        *, l):
    tq = tq_ref[...]
    q = _rot_a(q_ref[...], tq) * (HEAD_DIM_A ** -0.5)
    kn = _rot_a(kn_ref[...], tq)
    kr_ref[...] = kn
    k = jnp.concatenate([kcache_ref[...], kn], axis=0)
    v = jnp.concatenate([vcache_ref[...], vn_ref[...]], axis=0)
    _attend(q, k, v, None, sink_ref, l, o_ref, acc_ref)


def attn_sample(h, tab_a, sinks, cache_k, cache_v, l, S, nb, Ls):
    win = cache_k.shape[2]
    r0 = S // Ls
    return pl.pallas_call(
        functools.partial(_attn_sample_kernel, l=l),
        grid=(nb,),
        in_specs=[
            pl.BlockSpec(memory_space=pltpu.SMEM),
            pl.BlockSpec((Ls, A_WIDTH), lambda b: (r0 + b, 0)),
            pl.BlockSpec((Ls, A_KV_WIDTH), lambda b: (r0 + b, OFF_AK // A_KV_WIDTH)),
            pl.BlockSpec((Ls, A_KV_WIDTH), lambda b: (r0 + b, OFF_AV // A_KV_WIDTH)),
            pl.BlockSpec((None, None, win, A_KV_WIDTH), lambda b: (l, b, 0, 0)),
            pl.BlockSpec((None, None, win, A_KV_WIDTH), lambda b: (l, b, 0, 0)),
            pl.BlockSpec((Ls, 3 * LANES), lambda b: (r0 + b, 0)),
        ],
        out_specs=[
            pl.BlockSpec((Ls, A_WIDTH), lambda b: (b, 0)),
            pl.BlockSpec((Ls, A_KV_WIDTH), lambda b: (b, 0)),
        ],
        out_shape=[
            jax.ShapeDtypeStruct((nb * Ls, A_WIDTH), BF16),
            jax.ShapeDtypeStruct((nb * Ls, A_KV_WIDTH), F32),
        ],
        scratch_shapes=[pltpu.VMEM((Ls, A_WIDTH), F32)],
        compiler_params=_params("parallel"),
        name="attn_sample",
    )(sinks, h, h, h, cache_k, cache_v, tab_a)


def _rot_b(x, cs):
    half = B_HEAD_DIM // 2
    c, s = cs[:, :half], cs[:, half:]
    x1, x2 = x[:, :half], x[:, half:]
    return jnp.concatenate([x1 * c - x2 * s, x2 * c + x1 * s], axis=1)


def _retention_step(lg, q, k, v, g, cs, r):
    L = q.shape[0]
    q = _rot_b(q, cs)
    k = _rot_b(k, cs) * (B_HEAD_DIM ** -0.5)
    ii = lax.broadcasted_iota(jnp.int32, (L, L), 0)
    jj = lax.broadcasted_iota(jnp.int32, (L, L), 1)
    diff = (ii - jj).astype(F32)
    decay = jnp.where(diff >= 0, jnp.exp(lg * jnp.maximum(diff, 0.0)), 0.0)
    row = lax.broadcasted_iota(jnp.int32, (L, B_HEAD_DIM), 0).astype(F32)
    qb = q.astype(BF16)
    vb = v.astype(BF16)
    scores = lax.dot_general(qb, k.astype(BF16), (((1,), (1,)), ((), ())), preferred_element_type=F32) * decay
    inner = jnp.dot(scores.astype(BF16), vb, preferred_element_type=F32)
    cross = jnp.dot(qb, r.astype(BF16), preferred_element_type=F32) * jnp.exp(lg * (row + 1.0))
    k_dec = k * jnp.exp(lg * (L - 1.0 - row))
    g_all = jnp.exp(lg * jnp.full((1, B_HEAD_DIM), float(L), F32))
    r_new = r * g_all + jnp.dot(k_dec.T.astype(BF16), vb, preferred_element_type=F32)
    o = inner + cross
    on = o * lax.rsqrt(jnp.mean(o * o, -1, keepdims=True) + RMS_EPS)
    return on * (g * _sigmoid(g)), r_new


def _ret_prompt_kernel(lg_ref, q_ref, k_ref, v_ref, g_ref, cs_ref, o_ref, r_ref, r_scr):
    hd = pl.program_id(0)
    c = pl.program_id(1)

    @pl.when(c == 0)
    def _():
        r_scr[...] = jnp.zeros_like(r_scr)

    o, r_new = _retention_step(lg_ref[hd], q_ref[...], k_ref[...], v_ref[...], g_ref[...], cs_ref[...], r_scr[...])
    o_ref[...] = o.astype(o_ref.dtype)
    r_scr[...] = r_new
    r_ref[...] = r_new


def ret_prompt(h, tab_b, log_gamma, S):
    L = RET_BLOCK
    nc = S // L
    col = lambda off: (lambda hd, c: (c, off // B_HEAD_DIM + hd))
    return pl.pallas_call(
        _ret_prompt_kernel,
        grid=(B_HEADS, nc),
        in_specs=[
            pl.BlockSpec(memory_space=pltpu.SMEM),
            pl.BlockSpec((L, B_HEAD_DIM), col(OFF_BQ)),
            pl.BlockSpec((L, B_HEAD_DIM), col(OFF_BK)),
            pl.BlockSpec((L, B_HEAD_DIM), col(OFF_BV)),
            pl.BlockSpec((L, B_HEAD_DIM), col(OFF_BG)),
            pl.BlockSpec((L, B_HEAD_DIM), lambda hd, c: (c, 0)),
        ],
        out_specs=[
            pl.BlockSpec((L, B_HEAD_DIM), lambda hd, c: (c, hd)),
            pl.BlockSpec((None, B_HEAD_DIM, B_HEAD_DIM), lambda hd, c: (hd, 0, 0)),
        ],
        out_shape=[
            jax.ShapeDtypeStruct((S, B_WIDTH), BF16),
            jax.ShapeDtypeStruct((B_HEADS, B_HEAD_DIM, B_HEAD_DIM), F32),
        ],
        scratch_shapes=[pltpu.VMEM((B_HEAD_DIM, B_HEAD_DIM), F32)],
        compiler_params=_params("parallel", "arbitrary"),
        name="ret_prompt",
    )(log_gamma, h, h, h, h, tab_b)


def _ret_sample_kernel(lg_ref, q_ref, k_ref, v_ref, g_ref, cs_ref, r0_ref, o_ref, r_ref):
    hd = pl.program_id(1)
    o, r_new = _retention_step(lg_ref[hd], q_ref[...], k_ref[...], v_ref[...], g_ref[...], cs_ref[...], r0_ref[...])
    o_ref[...] = o.astype(o_ref.dtype)
    r_ref[...] = r_new


def ret_sample(h, tab_b, log_gamma, state_b, l, S, nb, Ls):
    r0 = S // Ls
    col = lambda off: (lambda b, hd: (r0 + b, off // B_HEAD_DIM + hd))
    return pl.pallas_call(
        _ret_sample_kernel,
        grid=(nb, B_HEADS),
        in_specs=[
            pl.BlockSpec(memory_space=pltpu.SMEM),
            pl.BlockSpec((Ls, B_HEAD_DIM), col(OFF_BQ)),
            pl.BlockSpec((Ls, B_HEAD_DIM), col(OFF_BK)),
            pl.BlockSpec((Ls, B_HEAD_DIM), col(OFF_BV)),
            pl.BlockSpec((Ls, B_HEAD_DIM), col(OFF_BG)),
            pl.BlockSpec((Ls, B_HEAD_DIM), lambda b, hd: (r0 + b, 0)),
            pl.BlockSpec((None, None, None, B_HEAD_DIM, B_HEAD_DIM), lambda b, hd: (l, b, hd, 0, 0)),
        ],
        out_specs=[
            pl.BlockSpec((Ls, B_HEAD_DIM), lambda b, hd: (b, hd)),
            pl.BlockSpec((None, None, B_HEAD_DIM, B_HEAD_DIM), lambda b, hd: (b, hd, 0, 0)),
        ],
        out_shape=[
            jax.ShapeDtypeStruct((nb * Ls, B_WIDTH), BF16),
            jax.ShapeDtypeStruct((nb, B_HEADS, B_HEAD_DIM, B_HEAD_DIM), F32),
        ],
        compiler_params=_params("parallel", "parallel"),
        name="ret_sample",
    )(log_gamma, h, h, h, h, tab_b, state_b)


def _cmlp_kernel(u0_ref, u1_ref, v0_ref, v1_ref, g_ref, b_ref, ws_ref, wbt_ref, o_ref, *maybe_vn_ref):
    R = u0_ref.shape[0]
    u = _gelu(jnp.concatenate([u0_ref[...], u1_ref[...]], axis=1))
    v = jnp.concatenate([v0_ref[...], v1_ref[...]], axis=1)
    vn = _layer_norm(_gelu(v), g_ref[...], b_ref[...])
    if maybe_vn_ref:
        maybe_vn_ref[0][...] = vn
    vb = vn.astype(BF16)
    ii = lax.broadcasted_iota(jnp.int32, (R, R), 0) // CHUNK
    jj = lax.broadcasted_iota(jnp.int32, (R, R), 1) // CHUNK
    for gi in range(C_GROUPS):
        wm = jnp.where(jj <= ii, ws_ref[gi][:R, :R], 0.0).astype(BF16)
        sl = slice(gi * C_GROUP_DIM, (gi + 1) * C_GROUP_DIM)
        sv = jnp.dot(wm, vb[:, sl], preferred_element_type=F32) + wbt_ref[:R, gi:gi + 1]
        o_ref[:, sl] = (u[:, sl] * sv).astype(o_ref.dtype)


def cmlp(h, c_ln_g3, c_ln_b3, c_ws, c_wbt, l, row0, n_blocks, R, with_vn):
    b0 = row0 // R
    hw = C_WIDTH // 2
    out_shape = [jax.ShapeDtypeStruct((n_blocks * R, C_WIDTH), BF16)]
    out_specs = [pl.BlockSpec((R, C_WIDTH), lambda i: (i, 0))]
    if with_vn:
        out_shape.append(jax.ShapeDtypeStruct((n_blocks * R, C_WIDTH), F32))
        out_specs.append(pl.BlockSpec((R, C_WIDTH), lambda i: (i, 0)))
    return pl.pallas_call(
        _cmlp_kernel,
        grid=(n_blocks,),
        in_specs=[
            pl.BlockSpec((R, hw), lambda i: (b0 + i, OFF_CU // hw)),
            pl.BlockSpec((R, hw), lambda i: (b0 + i, OFF_CU // hw + 1)),
            pl.BlockSpec((R, hw), lambda i: (b0 + i, OFF_CV // hw)),
            pl.BlockSpec((R, hw), lambda i: (b0 + i, OFF_CV // hw + 1)),
            pl.BlockSpec((None, 1, C_WIDTH), lambda i: (l, 0, 0)),
            pl.BlockSpec((None, 1, C_WIDTH), lambda i: (l, 0, 0)),
            pl.BlockSpec((None, C_GROUPS, CMLP_CHUNK, CMLP_CHUNK), lambda i: (l, 0, 0, 0)),
            pl.BlockSpec((None, CMLP_CHUNK, C_GROUPS), lambda i: (l, 0, 0)),
        ],
        out_specs=out_specs,
        out_shape=out_shape,
        compiler_params=_params("parallel"),
        name="cmlp_vn" if with_vn else "cmlp",
    )(h, h, h, h, c_ln_g3, c_ln_b3, c_ws, c_wbt)


def _split_bf16(x):
    hi = x.astype(BF16)
    lo = (x - hi.astype(F32)).astype(BF16)
    return hi, lo


def _norm1_router_kernel(x_ref, h_ref, g_ref, b_ref, rw_ref, rb_ref,
                         x1_ref, x1b_ref, gates_ref, idx_ref, p_ref, *, alpha):
    x1 = _layer_norm(alpha * x_ref[...] + h_ref[...], g_ref[...], b_ref[...])
    x1_ref[...] = x1
    x1b_ref[...] = x1.astype(BF16)
    xh, xl = _split_bf16(x1)
    wh, wl = _split_bf16(rw_ref[...])
    dot = functools.partial(jnp.dot, preferred_element_type=F32)
    logits = dot(xh, wh) + (dot(xh, wl) + dot(xl, wh)) + rb_ref[...]
    R = logits.shape[0]
    lane = lax.broadcasted_iota(jnp.int32, (R, N_EXPERTS), 1).astype(F32)
    lane_k = lax.broadcasted_iota(jnp.int32, (R, TOP_K), 1)
    work = logits
    vals, idxs = [], []
    for _ in range(TOP_K):
        m = jnp.max(work, -1, keepdims=True)
        ix = jnp.min(jnp.where(work == m, lane, float(N_EXPERTS)), -1, keepdims=True)
        vals.append(m)
        idxs.append(ix)
        work = jnp.where(lane == ix, -jnp.inf, work)
    es = [jnp.exp(v - vals[0]) for v in vals]
    tot = es[0] + es[1] + es[2] + es[3]
    gates = jnp.zeros((R, N_EXPERTS), F32)
    idx_out = jnp.zeros((R, TOP_K), jnp.int32)
    p_out = jnp.zeros((R, TOP_K), F32)
    for k in range(TOP_K):
        pk = es[k] / tot
        gates = jnp.where(lane == idxs[k], pk, gates)
        idx_out = jnp.where(lane_k == k, idxs[k].astype(jnp.int32), idx_out)
        p_out = jnp.where(lane_k == k, pk, p_out)
    gates_ref[...] = gates
    idx_ref[...] = idx_out
    p_ref[...] = p_out


def norm1_router(x, h2, ln_g3, ln_b3, router_w, router_b3, l, alpha):
    T = x.shape[0]
    tm = _row_tile(T, 256)
    row = lambda i: (i, 0)
    vec = lambda i: (l, 0, 0)
    return pl.pallas_call(
        functools.partial(_norm1_router_kernel, alpha=alpha),
        grid=(T // tm,),
        in_specs=[
            pl.BlockSpec((tm, D_MODEL), row),
            pl.BlockSpec((tm, D_MODEL), row),
            pl.BlockSpec((None, 1, D_MODEL), vec),
            pl.BlockSpec((None, 1, D_MODEL), vec),
            pl.BlockSpec((None, D_MODEL, N_EXPERTS), vec),
            pl.BlockSpec((None, 1, N_EXPERTS), vec),
        ],
        out_specs=[
            pl.BlockSpec((tm, D_MODEL), row),
            pl.BlockSpec((tm, D_MODEL), row),
            pl.BlockSpec((tm, N_EXPERTS), row),
            pl.BlockSpec((tm, TOP_K), row),
            pl.BlockSpec((tm, TOP_K), row),
        ],
        out_shape=[
            jax.ShapeDtypeStruct((T, D_MODEL), F32),
            jax.ShapeDtypeStruct((T, D_MODEL), BF16),
            jax.ShapeDtypeStruct((T, N_EXPERTS), F32),
            jax.ShapeDtypeStruct((T, TOP_K), jnp.int32),
            jax.ShapeDtypeStruct((T, TOP_K), F32),
        ],
        compiler_params=_params("parallel"),
        name="norm1_router",
    )(x, h2, ln_g3, ln_b3, router_w, router_b3)


def routing_plan(idx):
    T = idx.shape[0]
    n_pairs = T * TOP_K
    n_tiles = n_pairs // MOE_TILE + N_EXPERTS
    flat_e = idx.reshape(n_pairs)
    onehot = (flat_e[:, None] == jnp.arange(N_EXPERTS, dtype=jnp.int32)[None, :]).astype(jnp.int32)
    csum = jnp.cumsum(onehot, axis=0)
    rank = jnp.sum(csum * onehot, axis=1) - 1
    counts = csum[-1]
    padded = ((counts + MOE_TILE - 1) // MOE_TILE) * MOE_TILE
    pend = jnp.cumsum(padded)
    pstart = pend - padded
    dest = jnp.sum(onehot * pstart[None, :], axis=1) + rank
    src_tok = jnp.zeros((n_tiles * MOE_TILE,), jnp.int32).at[dest].set(
        jnp.arange(n_pairs, dtype=jnp.int32) // TOP_K)
    tile_start = jnp.arange(n_tiles, dtype=jnp.int32) * MOE_TILE
    tile_e = jnp.minimum(jnp.searchsorted(pend, tile_start, side="right"), N_EXPERTS - 1).astype(jnp.int32)
    n_used = (pend[-1] // MOE_TILE).astype(jnp.int32).reshape(1)
    return dest, src_tok, tile_e, n_used


def _expert_up_kernel(te_ref, nu_ref, x_ref, wg_ref, wu_ref, bg_ref, bu_ref, a_ref):
    m = pl.program_id(1)

    @pl.when(m < nu_ref[0])
    def _():
        x = x_ref[...]
        hg = jnp.dot(x, wg_ref[...].astype(BF16), preferred_element_type=F32) + bg_ref[...]
        hu = jnp.dot(x, wu_ref[...].astype(BF16), preferred_element_type=F32) + bu_ref[...]
        g = jnp.minimum(hg, SWIGLU_LIMIT)
        u = jnp.clip(hu, -SWIGLU_LIMIT, SWIGLU_LIMIT)
        a_ref[...] = ((u + 1.0) * g * _sigmoid(SWIGLU_ALPHA * g)).astype(a_ref.dtype)


def expert_up(xs, exp_w_gu, exp_b_gu4, tile_e, n_used, l):
    rows = xs.shape[0]
    n_tiles = rows // MOE_TILE
    tn = 512
    nj = EXPERT_DIM // tn
    mc = lambda m, nu: jnp.minimum(m, nu[0] - 1)
    return pl.pallas_call(
        _expert_up_kernel,
        grid_spec=pltpu.PrefetchScalarGridSpec(
            num_scalar_prefetch=2,
            grid=(nj, n_tiles),
            in_specs=[
                pl.BlockSpec((MOE_TILE, D_MODEL), lambda j, m, te, nu: (mc(m, nu), 0)),
                pl.BlockSpec((None, None, D_MODEL, tn), lambda j, m, te, nu: (l, te[mc(m, nu)], 0, j)),
                pl.BlockSpec((None, None, D_MODEL, tn), lambda j, m, te, nu: (l, te[mc(m, nu)], 0, nj + j)),
                pl.BlockSpec((None, None, 1, tn), lambda j, m, te, nu: (l, te[mc(m, nu)], 0, j)),
                pl.BlockSpec((None, None, 1, tn), lambda j, m, te, nu: (l, te[mc(m, nu)], 0, nj + j)),
            ],
            out_specs=pl.BlockSpec((MOE_TILE, tn), lambda j, m, te, nu: (mc(m, nu), j)),
        ),
        out_shape=jax.ShapeDtypeStruct((rows, EXPERT_DIM), BF16),
        compiler_params=_params("arbitrary", "arbitrary"),
        name="expert_up",
    )(tile_e, n_used, xs, exp_w_gu, exp_w_gu, exp_b_gu4, exp_b_gu4)


def _expert_down_kernel(te_ref, nu_ref, a_ref, w_ref, y_ref):
    m = pl.program_id(0)

    @pl.when(m < nu_ref[0])
    def _():
        y_ref[...] = jnp.dot(a_ref[...], w_ref[...].astype(BF16), preferred_element_type=F32)


def expert_down(a, exp_w_d, tile_e, n_used, l):
    rows = a.shape[0]
    n_tiles = rows // MOE_TILE
    mc = lambda m, nu: jnp.minimum(m, nu[0] - 1)
    return pl.pallas_call(
        _expert_down_kernel,
        grid_spec=pltpu.PrefetchScalarGridSpec(
            num_scalar_prefetch=2,
            grid=(n_tiles,),
            in_specs=[
                pl.BlockSpec((MOE_TILE, EXPERT_DIM), lambda m, te, nu: (mc(m, nu), 0)),
                pl.BlockSpec((None, None, EXPERT_DIM, D_MODEL), lambda m, te, nu: (l, te[mc(m, nu)], 0, 0)),
            ],
            out_specs=pl.BlockSpec((MOE_TILE, D_MODEL), lambda m, te, nu: (mc(m, nu), 0)),
        ),
        out_shape=jax.ShapeDtypeStruct((rows, D_MODEL), F32),
        compiler_params=_params("arbitrary"),
        name="expert_down",
    )(tile_e, n_used, a, exp_w_d)


def _combine_norm2_kernel(x1_ref, y0_ref, y1_ref, y2_ref, y3_ref, p_ref, gates_ref, bd_ref, g_ref, b_ref,
                          x2_ref, x2b_ref, *, alpha):
    p = p_ref[...]
    y = jnp.dot(gates_ref[...].astype(BF16), bd_ref[...].astype(BF16), preferred_element_type=F32)
    for k, yk_ref in enumerate((y0_ref, y1_ref, y2_ref, y3_ref)):
        y += p[:, k:k + 1] * yk_ref[...]
    x2 = _layer_norm(alpha * x1_ref[...] + y, g_ref[...], b_ref[...])
    x2_ref[...] = x2
    x2b_ref[...] = x2.astype(BF16)


def combine_norm2(x1, yk, p, gates, exp_b_d, ln_g3, ln_b3, l, alpha):
    T = x1.shape[0]
    tm = _row_tile(T, 128)
    row = lambda i: (i, 0)
    vec = lambda i: (l, 0, 0)
    return pl.pallas_call(
        functools.partial(_combine_norm2_kernel, alpha=alpha),
        grid=(T // tm,),
        in_specs=[
            pl.BlockSpec((tm, D_MODEL), row),
            pl.BlockSpec((None, tm, D_MODEL), lambda i: (0, i, 0)),
            pl.BlockSpec((None, tm, D_MODEL), lambda i: (1, i, 0)),
            pl.BlockSpec((None, tm, D_MODEL), lambda i: (2, i, 0)),
            pl.BlockSpec((None, tm, D_MODEL), lambda i: (3, i, 0)),
            pl.BlockSpec((tm, TOP_K), row),
            pl.BlockSpec((tm, N_EXPERTS), row),
            pl.BlockSpec((None, N_EXPERTS, D_MODEL), vec),
            pl.BlockSpec((None, 1, D_MODEL), vec),
            pl.BlockSpec((None, 1, D_MODEL), vec),
        ],
        out_specs=[pl.BlockSpec((tm, D_MODEL), row), pl.BlockSpec((tm, D_MODEL), row)],
        out_shape=[jax.ShapeDtypeStruct((T, D_MODEL), F32), jax.ShapeDtypeStruct((T, D_MODEL), BF16)],
        compiler_params=_params("parallel"),
        name="combine_norm2",
    )(x1, yk, yk, yk, yk, p, gates, exp_b_d, ln_g3, ln_b3)


def _ple_kernel(xb_ref, x_ref, p_ref, wg_ref, wp_ref, o_ref, ob_ref):
    gate = _sigmoid(jnp.dot(xb_ref[...], wg_ref[...].astype(BF16), preferred_element_type=F32))
    pe = jnp.dot(p_ref[...].astype(BF16), wp_ref[...].astype(BF16), preferred_element_type=F32)
    out = x_ref[...] + gate * pe
    o_ref[...] = out
    ob_ref[...] = out.astype(BF16)


def ple(x2, x2b, p_l, ple_gate_w, ple_w, l):
    T = x2.shape[0]
    tm = _row_tile(T, 1088)
    tn = 512
    return pl.pallas_call(
        _ple_kernel,
        grid=(T // tm, D_MODEL // tn),
        in_specs=[
            pl.BlockSpec((tm, D_MODEL), lambda i, j: (i, 0)),
            pl.BlockSpec((tm, tn), lambda i, j: (i, j)),
            pl.BlockSpec((tm, PLE_DIM), lambda i, j: (i, 0)),
            pl.BlockSpec((None, D_MODEL, tn), lambda i, j: (l, 0, j)),
            pl.BlockSpec((None, PLE_DIM, tn), lambda i, j: (l, 0, j)),
        ],
        out_specs=[pl.BlockSpec((tm, tn), lambda i, j: (i, j)), pl.BlockSpec((tm, tn), lambda i, j: (i, j))],
        out_shape=[jax.ShapeDtypeStruct((T, D_MODEL), F32), jax.ShapeDtypeStruct((T, D_MODEL), BF16)],
        compiler_params=_params("parallel", "parallel"),
        name="ple",
    )(x2b, x2, p_l, ple_gate_w, ple_w)


def _tables(pos):
    posf = pos.astype(F32)[:, None]
    half = ROT_DIM_A // 2
    inv = jnp.float32(ROPE_THETA_A) ** (-jnp.arange(half, dtype=F32) / half)
    ang = posf * inv[None, :]
    cos, sin = jnp.cos(ang), jnp.sin(ang)
    n = pos.shape[0]
    ones = jnp.ones((n, HEAD_DIM_A - ROT_DIM_A), F32)
    zeros = jnp.zeros((n, HEAD_DIM_A - ROT_DIM_A), F32)
    z8 = jnp.zeros((n, half), F32)
    c64 = jnp.concatenate([cos, cos, ones], 1)
    lo64 = jnp.concatenate([-sin, z8, zeros], 1)
    hi64 = jnp.concatenate([z8, sin, zeros], 1)
    tab_a = jnp.concatenate([c64, c64, lo64, lo64, hi64, hi64], 1)
    hb = B_HEAD_DIM // 2
    invb = jnp.float32(RET_THETA) ** (-jnp.arange(hb, dtype=F32) / hb)
    angb = posf * invb[None, :]
    tab_b = jnp.concatenate([jnp.cos(angb), jnp.sin(angb)], 1)
    return tab_a, tab_b


def kernel(x_prompt, x_sample, cache_a_k, cache_a_v, state_b, p_prompt, p_sample, w_in, b_in, a_sinks, c_ln_g, c_ln_b, c_ws, c_wb, w_out, b_out, ln1_g, ln1_b, router_w, router_b, exp_w_gu, exp_b_gu, exp_w_d, exp_b_d, ln2_g, ln2_b, ple_w, ple_gate_w):
    depth = w_in.shape[0]
    alpha = (2 * depth) ** 0.25
    bp, S, _ = x_prompt.shape
    nb, Ls, _ = x_sample.shape
    assert bp == 1 and Ls == CHUNK and S % RET_BLOCK == 0
    win = cache_a_k.shape[2]
    n_s = nb * Ls

    log_gamma = jnp.log(1.0 - 2.0 ** (-5.0 - jnp.arange(B_HEADS, dtype=F32)))
    pos = jnp.concatenate([jnp.arange(S), jnp.tile(PAST_LEN + jnp.arange(Ls), nb)])
    tab_a, tab_b = _tables(pos)

    x = jnp.concatenate([x_prompt.reshape(S, D_MODEL), x_sample.reshape(n_s, D_MODEL)], 0)
    xb = x.astype(BF16)
    cache_k = cache_a_k.reshape(depth, nb, win, A_KV_WIDTH)
    cache_v = cache_a_v.reshape(depth, nb, win, A_KV_WIDTH)
    vec3 = lambda a: a.reshape(depth, 1, a.shape[-1])
    b_in3, b_out3 = vec3(b_in), vec3(b_out)
    c_ln_g3, c_ln_b3 = vec3(c_ln_g), vec3(c_ln_b)
    ln1_g3, ln1_b3, ln2_g3, ln2_b3 = vec3(ln1_g), vec3(ln1_b), vec3(ln2_g), vec3(ln2_b)
    router_b3 = vec3(router_b)
    c_wbt = jnp.swapaxes(c_wb, 1, 2)
    exp_b_gu4 = exp_b_gu.reshape(depth, N_EXPERTS, 1, 2 * EXPERT_DIM)

    ak_p, av_p, rb_p, ak_s, av_s, rb_s, cv_s = [], [], [], [], [], [], []
    for l in range(depth):
        h = in_proj(xb, w_in, b_in3, l)
        oa_p, kr_p = attn_prompt(h, tab_a, a_sinks, l, S)
        oa_s, kr_s = attn_sample(h, tab_a, a_sinks, cache_k, cache_v, l, S, nb, Ls)
        ob_p, r_p = ret_prompt(h, tab_b, log_gamma, S)
        ob_s, r_s = ret_sample(h, tab_b, log_gamma, state_b, l, S, nb, Ls)
        (oc_p,) = cmlp(h, c_ln_g3, c_ln_b3, c_ws, c_wbt, l, 0, S // CMLP_CHUNK, CMLP_CHUNK, False)
        oc_s, vn_s = cmlp(h, c_ln_g3, c_ln_b3, c_ws, c_wbt, l, S, nb, Ls, True)
        oa = jnp.concatenate([oa_p, oa_s], 0)
        ob = jnp.concatenate([ob_p, ob_s], 0)
        oc = jnp.concatenate([oc_p, oc_s], 0)
        h2 = out_proj(oa, ob, oc, w_out, b_out3, l)
        x1, x1b, gates, idx, p = norm1_router(x, h2, ln1_g3, ln1_b3, router_w, router_b3, l, alpha)

        dest, src_tok, tile_e, n_used = routing_plan(idx)
        xs = jnp.take(x1b, src_tok, axis=0)
        a = expert_up(xs, exp_w_gu, exp_b_gu4, tile_e, n_used, l)
        ys = expert_down(a, exp_w_d, tile_e, n_used, l)
        dest_k = dest.reshape(S + n_s, TOP_K).T.reshape(-1)
        yk = jnp.take(ys, dest_k, axis=0).reshape(TOP_K, S + n_s, D_MODEL)
        x2, x2b = combine_norm2(x1, yk, p, gates, exp_b_d, ln2_g3, ln2_b3, l, alpha)

        p_l = jnp.concatenate([p_prompt[l].reshape(S, PLE_DIM), p_sample[l].reshape(n_s, PLE_DIM)], 0)
        x, xb = ple(x2, x2b, p_l, ple_gate_w, ple_w, l)

        ak_p.append(kr_p[S - win:].reshape(1, win, A_KV_HEADS, HEAD_DIM_A))
        av_p.append(h[S - win:S, OFF_AV:OFF_AV + A_KV_WIDTH].reshape(1, win, A_KV_HEADS, HEAD_DIM_A))
        rb_p.append(r_p[None])
        k_all = jnp.concatenate([cache_k[l], kr_s.reshape(nb, Ls, A_KV_WIDTH)], 1)
        v_all = jnp.concatenate([cache_v[l], h[S:, OFF_AV:OFF_AV + A_KV_WIDTH].reshape(nb, Ls, A_KV_WIDTH)], 1)
        ak_s.append(k_all[:, -win:].reshape(nb, win, A_KV_HEADS, HEAD_DIM_A))
        av_s.append(v_all[:, -win:].reshape(nb, win, A_KV_HEADS, HEAD_DIM_A))
        rb_s.append(r_s)
        cv_s.append(vn_s.reshape(nb, Ls, C_WIDTH))

    return (x[:S].reshape(1, S, D_MODEL), x[S:].reshape(nb, Ls, D_MODEL),
            jnp.stack(ak_p), jnp.stack(av_p), jnp.stack(rb_p),
            jnp.stack(ak_s), jnp.stack(av_s), jnp.stack(rb_s), jnp.stack(cv_s))
```

```python
import functools

import jax
import jax.numpy as jnp
from jax import lax
from jax.experimental import pallas as pl
from jax.experimental.pallas import tpu as pltpu

D_MODEL = 4096
CHUNK = 64
PAST_LEN = 4096
HEAD_DIM_A = 64
A_WIDTH = 1024
A_KV_HEADS = 4
A_GROUP = 4
A_KV_WIDTH = 256
ROT_DIM_A = 16
ROPE_THETA_A = 500000.0
B_HEAD_DIM = 256
B_WIDTH = 2048
B_HEADS = 8
RET_THETA = 10000.0
RET_BLOCK = 256
C_WIDTH = 1024
C_GROUPS = 4
C_GROUP_DIM = 256
CMLP_CHUNK = 128
OFF_AQ, OFF_AK, OFF_AV = 0, 1024, 1280
OFF_BQ, OFF_BK, OFF_BV, OFF_BG = 1536, 3584, 5632, 7680
OFF_CU, OFF_CV = 9728, 10752
IN_COLS = 11776
N_EXPERTS = 32
TOP_K = 4
EXPERT_DIM = 1024
SWIGLU_LIMIT = 7.0
SWIGLU_ALPHA = 1.702
PLE_DIM = 256
LN_EPS = 1e-5
RMS_EPS = 1e-6

LANES = 128
MOE_TILE = 256
VMEM_LIMIT = 56 * 1024 * 1024

BF16 = jnp.bfloat16
F32 = jnp.float32


def _params(*sem):
    return pltpu.CompilerParams(dimension_semantics=sem, vmem_limit_bytes=VMEM_LIMIT)


def _row_tile(n, target):
    best = None
    for t in range(16, min(n, target) + 1, 16):
        if n % t == 0:
            best = t
    assert best is not None, n
    return best


def _layer_norm(x, g, b):
    mu = jnp.mean(x, -1, keepdims=True)
    xc = x - mu
    var = jnp.mean(xc * xc, -1, keepdims=True)
    return xc * lax.rsqrt(var + LN_EPS) * g + b


def _gelu(x):
    c = 0.7978845608028654
    return 0.5 * x * (1.0 + jnp.tanh(c * (x + 0.044715 * (x * x * x))))


def _sigmoid(x):
    return 1.0 / (1.0 + jnp.exp(-x))


def _in_proj_kernel(x_ref, w_ref, b_ref, o_ref):
    w = w_ref[...].astype(BF16)
    o_ref[...] = jnp.dot(x_ref[...], w, preferred_element_type=F32) + b_ref[...]


def in_proj(xb, w_in, b_in3, l):
    T = xb.shape[0]
    n = w_in.shape[2]
    tm = _row_tile(T, 1088)
    tn = 512
    return pl.pallas_call(
        _in_proj_kernel,
        grid=(T // tm, n // tn),
        in_specs=[
            pl.BlockSpec((tm, D_MODEL), lambda i, j: (i, 0)),
            pl.BlockSpec((None, D_MODEL, tn), lambda i, j: (l, 0, j)),
            pl.BlockSpec((None, 1, tn), lambda i, j: (l, 0, j)),
        ],
        out_specs=pl.BlockSpec((tm, tn), lambda i, j: (i, j)),
        out_shape=jax.ShapeDtypeStruct((T, n), F32),
        compiler_params=_params("parallel", "parallel"),
        name="in_proj",
    )(xb, w_in, b_in3)


def _out_proj_kernel(oa_ref, ob_ref, oc_ref, w_ref, b_ref, o_ref):
    w = w_ref[...].astype(BF16)
    acc = jnp.dot(oa_ref[...], w[:A_WIDTH], preferred_element_type=F32)
    acc += jnp.dot(ob_ref[...], w[A_WIDTH:A_WIDTH + B_WIDTH], preferred_element_type=F32)
    acc += jnp.dot(oc_ref[...], w[A_WIDTH + B_WIDTH:], preferred_element_type=F32)
    o_ref[...] = acc + b_ref[...]


def out_proj(oa, ob, oc, w_out, b_out3, l):
    T = oa.shape[0]
    tm = _row_tile(T, 1088)
    tn = 512
    return pl.pallas_call(
        _out_proj_kernel,
        grid=(T // tm, D_MODEL // tn),
        in_specs=[
            pl.BlockSpec((tm, A_WIDTH), lambda i, j: (i, 0)),
            pl.BlockSpec((tm, B_WIDTH), lambda i, j: (i, 0)),
            pl.BlockSpec((tm, C_WIDTH), lambda i, j: (i, 0)),
            pl.BlockSpec((None, D_MODEL, tn), lambda i, j: (l, 0, j)),
            pl.BlockSpec((None, 1, tn), lambda i, j: (l, 0, j)),
        ],
        out_specs=pl.BlockSpec((tm, tn), lambda i, j: (i, j)),
        out_shape=jax.ShapeDtypeStruct((T, D_MODEL), F32),
        compiler_params=_params("parallel", "parallel"),
        name="out_proj",
    )(oa, ob, oc, w_out, b_out3)


def _rot_a(x, tab):
    c, s_lo, s_hi = tab[:, :LANES], tab[:, LANES:2 * LANES], tab[:, 2 * LANES:]
    half = ROT_DIM_A // 2
    outs = []
    for j in range(x.shape[1] // LANES):
        xc = x[:, j * LANES:(j + 1) * LANES]
        up = pltpu.roll(xc, LANES - half, 1)
        dn = pltpu.roll(xc, half, 1)
        outs.append(xc * c + up * s_lo + dn * s_hi)
    return outs[0] if len(outs) == 1 else jnp.concatenate(outs, axis=1)


def _attend(q, k, v, valid, sink_ref, l, o_ref, acc_ref):
    R = q.shape[0]
    qb = q.astype(BF16)
    kb = k.astype(BF16)
    vb = v.astype(BF16)
    grp = lax.broadcasted_iota(jnp.int32, (A_GROUP * R, 1), 0) // R
    for kh in range(A_KV_HEADS):
        base = kh * A_GROUP
        qs = jnp.concatenate(
            [qb[:, (base + g) * HEAD_DIM_A:(base + g + 1) * HEAD_DIM_A] for g in range(A_GROUP)], axis=0)
        k_h = kb[:, kh * HEAD_DIM_A:(kh + 1) * HEAD_DIM_A]
        v_h = vb[:, kh * HEAD_DIM_A:(kh + 1) * HEAD_DIM_A]
        s = lax.dot_general(qs, k_h, (((1,), (1,)), ((), ())), preferred_element_type=F32)
        if valid is not None:
            s = jnp.where(valid, s, -jnp.inf)
        sink = jnp.zeros((A_GROUP * R, 1), F32)
        for g in range(A_GROUP):
            sink = jnp.where(grp == g, sink_ref[l, base + g], sink)
        m = jnp.maximum(jnp.max(s, -1, keepdims=True), sink)
        e = jnp.exp(s - m)
        p = e / (jnp.sum(e, -1, keepdims=True) + jnp.exp(sink - m))
        o = jnp.dot(p.astype(BF16), v_h, preferred_element_type=F32)
        for g in range(A_GROUP):
            h = base + g
            acc_ref[:, h * HEAD_DIM_A:(h + 1) * HEAD_DIM_A] = o[g * R:(g + 1) * R]
    o_ref[...] = acc_ref[...].astype(o_ref.dtype)


def _attn_prompt_kernel(sink_ref, q_ref, kp_ref, kc_ref, vp_ref, vc_ref, tq_ref, tp_ref, o_ref, kr_ref, acc_ref,
                        *, l):
    i = pl.program_id(0)
    R = q_ref.shape[0]
    tq = tq_ref[...]
    q = _rot_a(q_ref[...], tq) * (HEAD_DIM_A ** -0.5)
    kc = _rot_a(kc_ref[...], tq)
    kp = _rot_a(kp_ref[...], tp_ref[...])
    kr_ref[...] = kc
    k = jnp.concatenate([kp, kc], axis=0)
    v = jnp.concatenate([vp_ref[...], vc_ref[...]], axis=0)
    n_q = A_GROUP * R
    qc = (lax.broadcasted_iota(jnp.int32, (n_q, 2 * R), 0) % R) // CHUNK
    col = lax.broadcasted_iota(jnp.int32, (n_q, 2 * R), 1)
    kc_idx = col // CHUNK - R // CHUNK
    lo = jnp.where(i == 0, 0, -(R // CHUNK))
    valid = (kc_idx <= qc) & (kc_idx >= jnp.maximum(qc - 2, lo))
    _attend(q, k, v, valid, sink_ref, l, o_ref, acc_ref)


def attn_prompt(h, tab_a, sinks, l, S):
    R = 2 * CHUNK
    nb = S // R
    prev = lambda i: jnp.maximum(i - 1, 0)
    return pl.pallas_call(
        functools.partial(_attn_prompt_kernel, l=l),
        grid=(nb,),
        in_specs=[
            pl.BlockSpec(memory_space=pltpu.SMEM),
            pl.BlockSpec((R, A_WIDTH), lambda i: (i, 0)),
            pl.BlockSpec((R, A_KV_WIDTH), lambda i: (prev(i), OFF_AK // A_KV_WIDTH)),
            pl.BlockSpec((R, A_KV_WIDTH), lambda i: (i, OFF_AK // A_KV_WIDTH)),
            pl.BlockSpec((R, A_KV_WIDTH), lambda i: (prev(i), OFF_AV // A_KV_WIDTH)),
            pl.BlockSpec((R, A_KV_WIDTH), lambda i: (i, OFF_AV // A_KV_WIDTH)),
            pl.BlockSpec((R, 3 * LANES), lambda i: (i, 0)),
            pl.BlockSpec((R, 3 * LANES), lambda i: (prev(i), 0)),
        ],
        out_specs=[
            pl.BlockSpec((R, A_WIDTH), lambda i: (i, 0)),
            pl.BlockSpec((R, A_KV_WIDTH), lambda i: (i, 0)),
        ],
        out_shape=[
            jax.ShapeDtypeStruct((S, A_WIDTH), BF16),
            jax.ShapeDtypeStruct((S, A_KV_WIDTH), F32),
        ],
        scratch_shapes=[pltpu.VMEM((R, A_WIDTH), F32)],
        compiler_params=_params("parallel"),
        name="attn_prompt",
    )(sinks, h, h, h, h, h, tab_a, tab_a)


def _attn_sample_kernel(sink_ref, q_ref, kn_ref, vn_ref, kcache_ref, vcache_ref, tq_ref, o_ref, kr_ref, acc_ref,
                        *, l):
    tq = tq_ref[...]
    q = _rot_a(q_ref[...], tq) * (HEAD_DIM_A ** -0.5)
    kn = _rot_a(kn_ref[...], tq)
    kr_ref[...] = kn
    k = jnp.concatenate([kcache_ref[...], kn], axis=0)
    v = jnp.concatenate([vcache_ref[...], vn_ref[...]], axis=0)
    _attend(q, k, v, None, sink_ref, l, o_ref, acc_ref)


def attn_sample(h, tab_a, sinks, cache_k, cache_v, l, S, nb, Ls):
    win = cache_k.shape[2]
    r0 = S // Ls
    return pl.pallas_call(
        functools.partial(_attn_sample_kernel, l=l),
        grid=(nb,),
        in_specs=[
            pl.BlockSpec(memory_space=pltpu.SMEM),
            pl.BlockSpec((Ls, A_WIDTH), lambda b: (r0 + b, 0)),
            pl.BlockSpec((Ls, A_KV_WIDTH), lambda b: (r0 + b, OFF_AK // A_KV_WIDTH)),
            pl.BlockSpec((Ls, A_KV_WIDTH), lambda b: (r0 + b, OFF_AV // A_KV_WIDTH)),
            pl.BlockSpec((None, None, win, A_KV_WIDTH), lambda b: (l, b, 0, 0)),
            pl.BlockSpec((None, None, win, A_KV_WIDTH), lambda b: (l, b, 0, 0)),
            pl.BlockSpec((Ls, 3 * LANES), lambda b: (r0 + b, 0)),
        ],
        out_specs=[
            pl.BlockSpec((Ls, A_WIDTH), lambda b: (b, 0)),
            pl.BlockSpec((Ls, A_KV_WIDTH), lambda b: (b, 0)),
        ],
        out_shape=[
            jax.ShapeDtypeStruct((nb * Ls, A_WIDTH), BF16),
            jax.ShapeDtypeStruct((nb * Ls, A_KV_WIDTH), F32),
        ],
        scratch_shapes=[pltpu.VMEM((Ls, A_WIDTH), F32)],
        compiler_params=_params("parallel"),
        name="attn_sample",
    )(sinks, h, h, h, cache_k, cache_v, tab_a)


def _rot_b(x, cs):
    half = B_HEAD_DIM // 2
    c, s = cs[:, :half], cs[:, half:]
    x1, x2 = x[:, :half], x[:, half:]
    return jnp.concatenate([x1 * c - x2 * s, x2 * c + x1 * s], axis=1)


def _retention_step(lg, q, k, v, g, cs, r):
    L = q.shape[0]
    q = _rot_b(q, cs)
    k = _rot_b(k, cs) * (B_HEAD_DIM ** -0.5)
    ii = lax.broadcasted_iota(jnp.int32, (L, L), 0)
    jj = lax.broadcasted_iota(jnp.int32, (L, L), 1)
    diff = (ii - jj).astype(F32)
    decay = jnp.where(diff >= 0, jnp.exp(lg * jnp.maximum(diff, 0.0)), 0.0)
    row = lax.broadcasted_iota(jnp.int32, (L, B_HEAD_DIM), 0).astype(F32)
    qb = q.astype(BF16)
    vb = v.astype(BF16)
    scores = lax.dot_general(qb, k.astype(BF16), (((1,), (1,)), ((), ())), preferred_element_type=F32) * decay
    inner = jnp.dot(scores.astype(BF16), vb, preferred_element_type=F32)
    cross = jnp.dot(qb, r.astype(BF16), preferred_element_type=F32) * jnp.exp(lg * (row + 1.0))
    k_dec = k * jnp.exp(lg * (L - 1.0 - row))
    g_all = jnp.exp(lg * jnp.full((1, B_HEAD_DIM), float(L), F32))
    r_new = r * g_all + jnp.dot(k_dec.T.astype(BF16), vb, preferred_element_type=F32)
    o = inner + cross
    on = o * lax.rsqrt(jnp.mean(o * o, -1, keepdims=True) + RMS_EPS)
    return on * (g * _sigmoid(g)), r_new


def _ret_prompt_kernel(lg_ref, q_ref, k_ref, v_ref, g_ref, cs_ref, o_ref, r_ref, r_scr):
    hd = pl.program_id(0)
    c = pl.program_id(1)

    @pl.when(c == 0)
    def _():
        r_scr[...] = jnp.zeros_like(r_scr)

    o, r_new = _retention_step(lg_ref[hd], q_ref[...], k_ref[...], v_ref[...], g_ref[...], cs_ref[...], r_scr[...])
    o_ref[...] = o.astype(o_ref.dtype)
    r_scr[...] = r_new
    r_ref[...] = r_new


def ret_prompt(h, tab_b, log_gamma, S):
    L = RET_BLOCK
    nc = S // L
    col = lambda off: (lambda hd, c: (c, off // B_HEAD_DIM + hd))
    return pl.pallas_call(
        _ret_prompt_kernel,
        grid=(B_HEADS, nc),
        in_specs=[
            pl.BlockSpec(memory_space=pltpu.SMEM),
            pl.BlockSpec((L, B_HEAD_DIM), col(OFF_BQ)),
            pl.BlockSpec((L, B_HEAD_DIM), col(OFF_BK)),
            pl.BlockSpec((L, B_HEAD_DIM), col(OFF_BV)),
            pl.BlockSpec((L, B_HEAD_DIM), col(OFF_BG)),
            pl.BlockSpec((L, B_HEAD_DIM), lambda hd, c: (c, 0)),
        ],
        out_specs=[
            pl.BlockSpec((L, B_HEAD_DIM), lambda hd, c: (c, hd)),
            pl.BlockSpec((None, B_HEAD_DIM, B_HEAD_DIM), lambda hd, c: (hd, 0, 0)),
        ],
        out_shape=[
            jax.ShapeDtypeStruct((S, B_WIDTH), BF16),
            jax.ShapeDtypeStruct((B_HEADS, B_HEAD_DIM, B_HEAD_DIM), F32),
        ],
        scratch_shapes=[pltpu.VMEM((B_HEAD_DIM, B_HEAD_DIM), F32)],
        compiler_params=_params("parallel", "arbitrary"),
        name="ret_prompt",
    )(log_gamma, h, h, h, h, tab_b)


def _ret_sample_kernel(lg_ref, q_ref, k_ref, v_ref, g_ref, cs_ref, r0_ref, o_ref, r_ref):
    hd = pl.program_id(1)
    o, r_new = _retention_step(lg_ref[hd], q_ref[...], k_ref[...], v_ref[...], g_ref[...], cs_ref[...], r0_ref[...])
    o_ref[...] = o.astype(o_ref.dtype)
    r_ref[...] = r_new


def ret_sample(h, tab_b, log_gamma, state_b, l, S, nb, Ls):
    r0 = S // Ls
    col = lambda off: (lambda b, hd: (r0 + b, off // B_HEAD_DIM + hd))
    return pl.pallas_call(
        _ret_sample_kernel,
        grid=(nb, B_HEADS),
        in_specs=[
            pl.BlockSpec(memory_space=pltpu.SMEM),
            pl.BlockSpec((Ls, B_HEAD_DIM), col(OFF_BQ)),
            pl.BlockSpec((Ls, B_HEAD_DIM), col(OFF_BK)),
            pl.BlockSpec((Ls, B_HEAD_DIM), col(OFF_BV)),
            pl.BlockSpec((Ls, B_HEAD_DIM), col(OFF_BG)),
            pl.BlockSpec((Ls, B_HEAD_DIM), lambda b, hd: (r0 + b, 0)),
            pl.BlockSpec((None, None, None, B_HEAD_DIM, B_HEAD_DIM), lambda b, hd: (l, b, hd, 0, 0)),
        ],
        out_specs=[
            pl.BlockSpec((Ls, B_HEAD_DIM), lambda b, hd: (b, hd)),
            pl.BlockSpec((None, None, B_HEAD_DIM, B_HEAD_DIM), lambda b, hd: (b, hd, 0, 0)),
        ],
        out_shape=[
            jax.ShapeDtypeStruct((nb * Ls, B_WIDTH), BF16),
            jax.ShapeDtypeStruct((nb, B_HEADS, B_HEAD_DIM, B_HEAD_DIM), F32),
        ],
        compiler_params=_params("parallel", "parallel"),
        name="ret_sample",
    )(log_gamma, h, h, h, h, tab_b, state_b)


def _cmlp_kernel(u0_ref, u1_ref, v0_ref, v1_ref, g_ref, b_ref, ws_ref, wbt_ref, o_ref, *maybe_vn_ref):
    R = u0_ref.shape[0]
    u = _gelu(jnp.concatenate([u0_ref[...], u1_ref[...]], axis=1))
    v = jnp.concatenate([v0_ref[...], v1_ref[...]], axis=1)
    vn = _layer_norm(_gelu(v), g_ref[...], b_ref[...])
    if maybe_vn_ref:
        maybe_vn_ref[0][...] = vn
    vb = vn.astype(BF16)
    ii = lax.broadcasted_iota(jnp.int32, (R, R), 0) // CHUNK
    jj = lax.broadcasted_iota(jnp.int32, (R, R), 1) // CHUNK
    for gi in range(C_GROUPS):
        wm = jnp.where(jj <= ii, ws_ref[gi][:R, :R], 0.0).astype(BF16)
        sl = slice(gi * C_GROUP_DIM, (gi + 1) * C_GROUP_DIM)
        sv = jnp.dot(wm, vb[:, sl], preferred_element_type=F32) + wbt_ref[:R, gi:gi + 1]
        o_ref[:, sl] = (u[:, sl] * sv).astype(o_ref.dtype)


def cmlp(h, c_ln_g3, c_ln_b3, c_ws, c_wbt, l, row0, n_blocks, R, with_vn):
    b0 = row0 // R
    hw = C_WIDTH // 2
    out_shape = [jax.ShapeDtypeStruct((n_blocks * R, C_WIDTH), BF16)]
    out_specs = [pl.BlockSpec((R, C_WIDTH), lambda i: (i, 0))]
    if with_vn:
        out_shape.append(jax.ShapeDtypeStruct((n_blocks * R, C_WIDTH), F32))
        out_specs.append(pl.BlockSpec((R, C_WIDTH), lambda i: (i, 0)))
    return pl.pallas_call(
        _cmlp_kernel,
        grid=(n_blocks,),
        in_specs=[
            pl.BlockSpec((R, hw), lambda i: (b0 + i, OFF_CU // hw)),
            pl.BlockSpec((R, hw), lambda i: (b0 + i, OFF_CU // hw + 1)),
            pl.BlockSpec((R, hw), lambda i: (b0 + i, OFF_CV // hw)),
            pl.BlockSpec((R, hw), lambda i: (b0 + i, OFF_CV // hw + 1)),
            pl.BlockSpec((None, 1, C_WIDTH), lambda i: (l, 0, 0)),
            pl.BlockSpec((None, 1, C_WIDTH), lambda i: (l, 0, 0)),
            pl.BlockSpec((None, C_GROUPS, CMLP_CHUNK, CMLP_CHUNK), lambda i: (l, 0, 0, 0)),
            pl.BlockSpec((None, CMLP_CHUNK, C_GROUPS), lambda i: (l, 0, 0)),
        ],
        out_specs=out_specs,
        out_shape=out_shape,
        compiler_params=_params("parallel"),
        name="cmlp_vn" if with_vn else "cmlp",
    )(h, h, h, h, c_ln_g3, c_ln_b3, c_ws, c_wbt)


def _split_bf16(x):
    hi = x.astype(BF16)
    lo = (x - hi.astype(F32)).astype(BF16)
    return hi, lo


def _norm1_router_kernel(x_ref, h_ref, g_ref, b_ref, rw_ref, rb_ref,
                         x1_ref, gates_ref, idx_ref, p_ref, rank_ref, counts_ref, cnt_scr, *, alpha):
    @pl.when(pl.program_id(0) == 0)
    def _():
        cnt_scr[...] = jnp.zeros_like(cnt_scr)

    x1 = _layer_norm(alpha * x_ref[...] + h_ref[...], g_ref[...], b_ref[...])
    x1_ref[...] = x1
    xh, xl = _split_bf16(x1)
    wh, wl = _split_bf16(rw_ref[...])
    dot = functools.partial(jnp.dot, preferred_element_type=F32)
    logits = dot(xh, wh) + (dot(xh, wl) + dot(xl, wh)) + rb_ref[...]
    R = logits.shape[0]
    lane = lax.broadcasted_iota(jnp.int32, (R, N_EXPERTS), 1).astype(F32)
    lane_k = lax.broadcasted_iota(jnp.int32, (R, TOP_K), 1)
    work = logits
    vals, idxs = [], []
    for _ in range(TOP_K):
        m = jnp.max(work, -1, keepdims=True)
        ix = jnp.min(jnp.where(work == m, lane, float(N_EXPERTS)), -1, keepdims=True)
        vals.append(m)
        idxs.append(ix)
        work = jnp.where(lane == ix, -jnp.inf, work)
    es = [jnp.exp(v - vals[0]) for v in vals]
    tot = es[0] + es[1] + es[2] + es[3]
    gates = jnp.zeros((R, N_EXPERTS), F32)
    idx_out = jnp.zeros((R, TOP_K), jnp.int32)
    p_out = jnp.zeros((R, TOP_K), F32)
    for k in range(TOP_K):
        pk = es[k] / tot
        gates = jnp.where(lane == idxs[k], pk, gates)
        idx_out = jnp.where(lane_k == k, idxs[k].astype(jnp.int32), idx_out)
        p_out = jnp.where(lane_k == k, pk, p_out)
    gates_ref[...] = gates
    idx_ref[...] = idx_out
    p_ref[...] = p_out
    onehot = jnp.zeros((R, N_EXPERTS), F32)
    for k in range(TOP_K):
        onehot = jnp.where(lane == idxs[k], 1.0, onehot)
    before = lax.broadcasted_iota(jnp.int32, (R, R), 0) > lax.broadcasted_iota(jnp.int32, (R, R), 1)
    prefix = jnp.dot(jnp.where(before, 1.0, 0.0).astype(BF16), onehot.astype(BF16), preferred_element_type=F32)
    base = cnt_scr[...] + prefix
    rank_out = jnp.zeros((R, TOP_K), jnp.int32)
    for k in range(TOP_K):
        rk = jnp.sum(jnp.where(lane == idxs[k], base, 0.0), -1, keepdims=True)
        rank_out = jnp.where(lane_k == k, rk.astype(jnp.int32), rank_out)
    rank_ref[...] = rank_out
    cnt_scr[...] += jnp.sum(onehot, 0, keepdims=True)
    counts_ref[...] = cnt_scr[...]


def norm1_router(x, h2, ln_g3, ln_b3, router_w, router_b3, l, alpha):
    T = x.shape[0]
    tm = _row_tile(T, 256)
    row = lambda i: (i, 0)
    vec = lambda i: (l, 0, 0)
    return pl.pallas_call(
        functools.partial(_norm1_router_kernel, alpha=alpha),
        grid=(T // tm,),
        in_specs=[
            pl.BlockSpec((tm, D_MODEL), row),
            pl.BlockSpec((tm, D_MODEL), row),
            pl.BlockSpec((None, 1, D_MODEL), vec),
            pl.BlockSpec((None, 1, D_MODEL), vec),
            pl.BlockSpec((None, D_MODEL, N_EXPERTS), vec),
            pl.BlockSpec((None, 1, N_EXPERTS), vec),
        ],
        out_specs=[
            pl.BlockSpec((tm, D_MODEL), row),
            pl.BlockSpec((tm, N_EXPERTS), row),
            pl.BlockSpec((tm, TOP_K), row),
            pl.BlockSpec((tm, TOP_K), row),
            pl.BlockSpec((tm, TOP_K), row),
            pl.BlockSpec((1, N_EXPERTS), lambda i: (0, 0)),
        ],
        out_shape=[
            jax.ShapeDtypeStruct((T, D_MODEL), F32),
            jax.ShapeDtypeStruct((T, N_EXPERTS), F32),
            jax.ShapeDtypeStruct((T, TOP_K), jnp.int32),
            jax.ShapeDtypeStruct((T, TOP_K), F32),
            jax.ShapeDtypeStruct((T, TOP_K), jnp.int32),
            jax.ShapeDtypeStruct((1, N_EXPERTS), F32),
        ],
        scratch_shapes=[pltpu.VMEM((1, N_EXPERTS), F32)],
        compiler_params=_params("arbitrary"),
        name="norm1_router",
    )(x, h2, ln_g3, ln_b3, router_w, router_b3)


def routing_plan(idx, rank, counts):
    T = idx.shape[0]
    n_tiles = T * TOP_K // MOE_TILE + N_EXPERTS
    counts = counts.reshape(N_EXPERTS).astype(jnp.int32)
    padded = ((counts + MOE_TILE - 1) // MOE_TILE) * MOE_TILE
    pend = jnp.cumsum(padded)
    pstart = pend - padded
    experts = jnp.arange(N_EXPERTS, dtype=jnp.int32)
    dest = jnp.sum(jnp.where(idx[:, :, None] == experts, pstart, 0), -1) + rank
    tile_start = jnp.arange(n_tiles, dtype=jnp.int32) * MOE_TILE
    tile_e = jnp.minimum(jnp.sum((pend[None, :] <= tile_start[:, None]).astype(jnp.int32), 1), N_EXPERTS - 1)
    n_used = pend[-1] // MOE_TILE
    pad_row = jnp.where(padded > 0, pend - MOE_TILE, -1).astype(jnp.int32)
    tile = jnp.arange(n_tiles, dtype=jnp.int32)
    prev_e = jnp.concatenate([jnp.full((1,), -1, jnp.int32), tile_e[:-1]])
    first = ((tile < n_used) & (tile_e != prev_e)).astype(jnp.int32)
    run_end = jnp.sum(jnp.where(tile_e[:, None] == experts, pend // MOE_TILE, 0), -1)
    wraps = (run_end >= n_used).astype(jnp.int32)
    nxt_tile = jnp.where(wraps == 1, 0, jnp.minimum(run_end, n_tiles - 1))
    nxt_e = jnp.sum(jnp.where(nxt_tile[:, None] == tile[None, :], tile_e[None, :], 0), -1)
    plan = (tile_e, n_used.astype(jnp.int32).reshape(1), first, nxt_e.astype(jnp.int32), wraps)
    return dest, pad_row, plan


def _dma_rows(n_rows, body):
    unroll = 8
    assert n_rows % unroll == 0

    def trip(t, carry):
        for u in range(unroll):
            body(t * unroll + u)
        return carry

    lax.fori_loop(0, n_rows // unroll, trip, 0)


def _dispatch_kernel(pad_ref, nu_ref, dest_ref, x_ref, xs_hbm, zeros_buf, zero_sem, row_sem):
    tm = x_ref.shape[0]
    n_tiles = xs_hbm.shape[0] // MOE_TILE

    def zero_copy(row):
        start = pl.multiple_of(row, MOE_TILE)
        return pltpu.make_async_copy(zeros_buf, xs_hbm.at[pl.ds(start, MOE_TILE), :], zero_sem)

    @pl.when(pl.program_id(0) == 0)
    def _():
        zeros_buf[...] = jnp.zeros_like(zeros_buf)
        for e in range(N_EXPERTS):
            @pl.when(pad_ref[e] >= 0)
            def _():
                zero_copy(pad_ref[e]).start()

        def start_tail(t, c):
            zero_copy(t * MOE_TILE).start()
            return c

        def wait_tail(t, c):
            zero_copy(t * MOE_TILE).wait()
            return c

        lax.fori_loop(nu_ref[0], n_tiles, start_tail, 0)
        for e in range(N_EXPERTS):
            @pl.when(pad_ref[e] >= 0)
            def _():
                zero_copy(pad_ref[e]).wait()
        lax.fori_loop(nu_ref[0], n_tiles, wait_tail, 0)

    def row_copy(r, d):
        return pltpu.make_async_copy(x_ref.at[pl.ds(r, 1), :], xs_hbm.at[pl.ds(d, 1), :], row_sem)

    def start_row(r):
        for k in range(TOP_K):
            row_copy(r, dest_ref[0, r * TOP_K + k]).start()

    _dma_rows(tm, start_row)
    for k in range(TOP_K):
        pltpu.make_async_copy(x_ref, xs_hbm.at[pl.ds(0, tm), :], row_sem).wait()


def dispatch(x1, dest, pad_row, n_used):
    T = x1.shape[0]
    tm = _row_tile(T, 256)
    n_rows = (T * TOP_K // MOE_TILE + N_EXPERTS) * MOE_TILE
    dest3 = dest.reshape(T // tm, 1, tm * TOP_K)
    return pl.pallas_call(
        _dispatch_kernel,
        grid_spec=pltpu.PrefetchScalarGridSpec(
            num_scalar_prefetch=2,
            grid=(T // tm,),
            in_specs=[
                pl.BlockSpec((None, 1, tm * TOP_K), lambda i, *_: (i, 0, 0), memory_space=pltpu.SMEM),
                pl.BlockSpec((tm, D_MODEL), lambda i, *_: (i, 0)),
            ],
            out_specs=pl.BlockSpec(memory_space=pl.ANY),
            scratch_shapes=[
                pltpu.VMEM((MOE_TILE, D_MODEL), F32),
                pltpu.SemaphoreType.DMA(()),
                pltpu.SemaphoreType.DMA(()),
            ],
        ),
        out_shape=jax.ShapeDtypeStruct((n_rows, D_MODEL), F32),
        compiler_params=_params("arbitrary"),
        name="dispatch",
    )(pad_row, n_used, dest3, x1)


def _switch_weights(first, very_first, own_e, own_pass, nxt_e, nxt_pass, has_next, slot_ref, copies):
    @pl.when(very_first)
    def _():
        slot_ref[0] = 1
        for c in copies(own_e, own_pass, 0):
            c.start()

    @pl.when(first)
    def _():
        slot = 1 - slot_ref[0]
        slot_ref[0] = slot
        for c in copies(own_e, own_pass, slot):
            c.wait()

        @pl.when(has_next)
        def _():
            for c in copies(nxt_e, nxt_pass, 1 - slot):
                c.start()


def _expert_up_kernel(te_ref, nu_ref, first_ref, nxt_ref, wraps_ref, x_ref, wgu_hbm, bg_ref, bu_ref, a_ref,
                      wbuf, wsem, slot_ref, *, l, tn):
    j = pl.program_id(0)
    m = pl.program_id(1)
    nj = pl.num_programs(0)

    def copies(e, jj, slot):
        cols = lambda part: pl.ds(pl.multiple_of(part * EXPERT_DIM + jj * tn, tn), tn)
        return [pltpu.make_async_copy(wgu_hbm.at[l, e, :, cols(part)], wbuf.at[slot, part], wsem.at[slot, part])
                for part in range(2)]

    @pl.when(m < nu_ref[0])
    def _():
        _switch_weights(first_ref[m] == 1, (j == 0) & (m == 0), te_ref[m], j,
                        nxt_ref[m], j + wraps_ref[m], j + wraps_ref[m] < nj, slot_ref, copies)
        slot = slot_ref[0]
        x = x_ref[...].astype(BF16)
        hg = jnp.dot(x, wbuf[slot, 0].astype(BF16), preferred_element_type=F32) + bg_ref[...]
        hu = jnp.dot(x, wbuf[slot, 1].astype(BF16), preferred_element_type=F32) + bu_ref[...]
        g = jnp.minimum(hg, SWIGLU_LIMIT)
        u = jnp.clip(hu, -SWIGLU_LIMIT, SWIGLU_LIMIT)
        a_ref[...] = ((u + 1.0) * g * _sigmoid(SWIGLU_ALPHA * g)).astype(a_ref.dtype)

    @pl.when(m >= nu_ref[0])
    def _():
        a_ref[...] = jnp.zeros_like(a_ref)


def expert_up(xs, exp_w_gu, exp_b_gu4, plan, l):
    tile_e, n_used, first, nxt_e, wraps = plan
    rows = xs.shape[0]
    n_tiles = rows // MOE_TILE
    tn = 512
    nj = EXPERT_DIM // tn
    mc = lambda m, nu: jnp.minimum(m, nu[0] - 1)
    return pl.pallas_call(
        functools.partial(_expert_up_kernel, l=l, tn=tn),
        grid_spec=pltpu.PrefetchScalarGridSpec(
            num_scalar_prefetch=5,
            grid=(nj, n_tiles),
            in_specs=[
                pl.BlockSpec((MOE_TILE, D_MODEL), lambda j, m, te, nu, *_: (mc(m, nu), 0)),
                pl.BlockSpec(memory_space=pl.ANY),
                pl.BlockSpec((None, None, 1, tn), lambda j, m, te, nu, *_: (l, te[mc(m, nu)], 0, j)),
                pl.BlockSpec((None, None, 1, tn), lambda j, m, te, nu, *_: (l, te[mc(m, nu)], 0, nj + j)),
            ],
            out_specs=pl.BlockSpec((MOE_TILE, tn), lambda j, m, *_: (m, j)),
            scratch_shapes=[
                pltpu.VMEM((2, 2, D_MODEL, tn), F32),
                pltpu.SemaphoreType.DMA((2, 2)),
                pltpu.SMEM((1,), jnp.int32),
            ],
        ),
        out_shape=jax.ShapeDtypeStruct((rows, EXPERT_DIM), BF16),
        compiler_params=_params("arbitrary", "arbitrary"),
        name="expert_up",
    )(tile_e, n_used, first, nxt_e, wraps, xs, exp_w_gu, exp_b_gu4, exp_b_gu4)


def _expert_down_kernel(te_ref, nu_ref, first_ref, nxt_ref, wraps_ref, a_ref, wd_hbm, y_ref,
                        wbuf, wsem, slot_ref, *, l):
    m = pl.program_id(0)

    def copies(e, _, slot):
        return [pltpu.make_async_copy(wd_hbm.at[l, e], wbuf.at[slot], wsem.at[slot])]

    @pl.when(m < nu_ref[0])
    def _():
        _switch_weights(first_ref[m] == 1, m == 0, te_ref[m], 0, nxt_ref[m], 0, wraps_ref[m] == 0,
                        slot_ref, copies)
        y_ref[...] = jnp.dot(a_ref[...], wbuf[slot_ref[0]].astype(BF16), preferred_element_type=F32)

    @pl.when(m >= nu_ref[0])
    def _():
        y_ref[...] = jnp.zeros_like(y_ref)


def expert_down(a, exp_w_d, plan, l):
    tile_e, n_used, first, nxt_e, wraps = plan
    rows = a.shape[0]
    n_tiles = rows // MOE_TILE
    mc = lambda m, nu: jnp.minimum(m, nu[0] - 1)
    return pl.pallas_call(
        functools.partial(_expert_down_kernel, l=l),
        grid_spec=pltpu.PrefetchScalarGridSpec(
            num_scalar_prefetch=5,
            grid=(n_tiles,),
            in_specs=[
                pl.BlockSpec((MOE_TILE, EXPERT_DIM), lambda m, te, nu, *_: (mc(m, nu), 0)),
                pl.BlockSpec(memory_space=pl.ANY),
            ],
            out_specs=pl.BlockSpec((MOE_TILE, D_MODEL), lambda m, *_: (m, 0)),
            scratch_shapes=[
                pltpu.VMEM((2, EXPERT_DIM, D_MODEL), F32),
                pltpu.SemaphoreType.DMA((2,)),
                pltpu.SMEM((1,), jnp.int32),
            ],
        ),
        out_shape=jax.ShapeDtypeStruct((rows, D_MODEL), F32),
        compiler_params=_params("arbitrary"),
        name="expert_down",
    )(tile_e, n_used, first, nxt_e, wraps, a, exp_w_d)


def _combine_norm2_kernel(dcur_ref, dnext_ref, x1_ref, ys_hbm, p_ref, gates_ref, bd_ref, g_ref, b_ref,
                          x2_ref, x2b_ref, rows_buf, sems, *, alpha):
    i = pl.program_id(0)
    n = pl.num_programs(0)
    tm = x1_ref.shape[0]
    slot = i % 2

    def gather(d_ref, s):
        def start_row(r):
            pltpu.make_async_copy(ys_hbm.at[pl.ds(d_ref[0, r], 1), :], rows_buf.at[s, pl.ds(r, 1), :],
                                  sems.at[s]).start()
        _dma_rows(TOP_K * tm, start_row)

    @pl.when(i == 0)
    def _():
        gather(dcur_ref, 0)

    @pl.when(i + 1 < n)
    def _():
        gather(dnext_ref, 1 - slot)

    pltpu.make_async_copy(ys_hbm.at[pl.ds(0, TOP_K * tm), :], rows_buf.at[slot], sems.at[slot]).wait()
    p = p_ref[...]
    y = jnp.dot(gates_ref[...].astype(BF16), bd_ref[...].astype(BF16), preferred_element_type=F32)
    for k in range(TOP_K):
        y += p[:, k:k + 1] * rows_buf[slot, k * tm:(k + 1) * tm, :]
    x2 = _layer_norm(alpha * x1_ref[...] + y, g_ref[...], b_ref[...])
    x2_ref[...] = x2
    x2b_ref[...] = x2.astype(BF16)


def combine_norm2(x1, ys, dest, p, gates, exp_b_d, ln_g3, ln_b3, l, alpha):
    T = x1.shape[0]
    tm = _row_tile(T, 128)
    nt = T // tm
    row = lambda i: (i, 0)
    vec = lambda i: (l, 0, 0)
    dest3 = dest.reshape(nt, tm, TOP_K).transpose(0, 2, 1).reshape(nt, 1, TOP_K * tm)
    return pl.pallas_call(
        functools.partial(_combine_norm2_kernel, alpha=alpha),
        grid=(nt,),
        in_specs=[
            pl.BlockSpec((None, 1, TOP_K * tm), lambda i: (i, 0, 0), memory_space=pltpu.SMEM),
            pl.BlockSpec((None, 1, TOP_K * tm), lambda i: (jnp.minimum(i + 1, nt - 1), 0, 0),
                         memory_space=pltpu.SMEM),
            pl.BlockSpec((tm, D_MODEL), row),
            pl.BlockSpec(memory_space=pl.ANY),
            pl.BlockSpec((tm, TOP_K), row),
            pl.BlockSpec((tm, N_EXPERTS), row),
            pl.BlockSpec((None, N_EXPERTS, D_MODEL), vec),
            pl.BlockSpec((None, 1, D_MODEL), vec),
            pl.BlockSpec((None, 1, D_MODEL), vec),
        ],
        out_specs=[pl.BlockSpec((tm, D_MODEL), row), pl.BlockSpec((tm, D_MODEL), row)],
        out_shape=[jax.ShapeDtypeStruct((T, D_MODEL), F32), jax.ShapeDtypeStruct((T, D_MODEL), BF16)],
        scratch_shapes=[
            pltpu.VMEM((2, TOP_K * tm, D_MODEL), F32),
            pltpu.SemaphoreType.DMA((2,)),
        ],
        compiler_params=_params("arbitrary"),
        name="combine_norm2",
    )(dest3, dest3, x1, ys, p, gates, exp_b_d, ln_g3, ln_b3)


def _ple_kernel(xb_ref, x_ref, p_ref, wg_ref, wp_ref, o_ref, ob_ref):
    gate = _sigmoid(jnp.dot(xb_ref[...], wg_ref[...].astype(BF16), preferred_element_type=F32))
    pe = jnp.dot(p_ref[...].astype(BF16), wp_ref[...].astype(BF16), preferred_element_type=F32)
    out = x_ref[...] + gate * pe
    o_ref[...] = out
    ob_ref[...] = out.astype(BF16)


def ple(x2, x2b, p_l, ple_gate_w, ple_w, l):
    T = x2.shape[0]
    tm = _row_tile(T, 1088)
    tn = 512
    return pl.pallas_call(
        _ple_kernel,
        grid=(T // tm, D_MODEL // tn),
        in_specs=[
            pl.BlockSpec((tm, D_MODEL), lambda i, j: (i, 0)),
            pl.BlockSpec((tm, tn), lambda i, j: (i, j)),
            pl.BlockSpec((tm, PLE_DIM), lambda i, j: (i, 0)),
            pl.BlockSpec((None, D_MODEL, tn), lambda i, j: (l, 0, j)),
            pl.BlockSpec((None, PLE_DIM, tn), lambda i, j: (l, 0, j)),
        ],
        out_specs=[pl.BlockSpec((tm, tn), lambda i, j: (i, j)), pl.BlockSpec((tm, tn), lambda i, j: (i, j))],
        out_shape=[jax.ShapeDtypeStruct((T, D_MODEL), F32), jax.ShapeDtypeStruct((T, D_MODEL), BF16)],
        compiler_params=_params("parallel", "parallel"),
        name="ple",
    )(x2b, x2, p_l, ple_gate_w, ple_w)


def _tables(pos):
    posf = pos.astype(F32)[:, None]
    half = ROT_DIM_A // 2
    inv = jnp.float32(ROPE_THETA_A) ** (-jnp.arange(half, dtype=F32) / half)
    ang = posf * inv[None, :]
    cos, sin = jnp.cos(ang), jnp.sin(ang)
    n = pos.shape[0]
    ones = jnp.ones((n, HEAD_DIM_A - ROT_DIM_A), F32)
    zeros = jnp.zeros((n, HEAD_DIM_A - ROT_DIM_A), F32)
    z8 = jnp.zeros((n, half), F32)
    c64 = jnp.concatenate([cos, cos, ones], 1)
    lo64 = jnp.concatenate([-sin, z8, zeros], 1)
    hi64 = jnp.concatenate([z8, sin, zeros], 1)
    tab_a = jnp.concatenate([c64, c64, lo64, lo64, hi64, hi64], 1)
    hb = B_HEAD_DIM // 2
    invb = jnp.float32(RET_THETA) ** (-jnp.arange(hb, dtype=F32) / hb)
    angb = posf * invb[None, :]
    tab_b = jnp.concatenate([jnp.cos(angb), jnp.sin(angb)], 1)
    return tab_a, tab_b


def kernel(x_prompt, x_sample, cache_a_k, cache_a_v, state_b, p_prompt, p_sample, w_in, b_in, a_sinks, c_ln_g, c_ln_b, c_ws, c_wb, w_out, b_out, ln1_g, ln1_b, router_w, router_b, exp_w_gu, exp_b_gu, exp_w_d, exp_b_d, ln2_g, ln2_b, ple_w, ple_gate_w):
    depth = w_in.shape[0]
    alpha = (2 * depth) ** 0.25
    bp, S, _ = x_prompt.shape
    nb, Ls, _ = x_sample.shape
    assert bp == 1 and Ls == CHUNK and S % RET_BLOCK == 0
    win = cache_a_k.shape[2]
    n_s = nb * Ls

    log_gamma = jnp.log(1.0 - 2.0 ** (-5.0 - jnp.arange(B_HEADS, dtype=F32)))
    pos = jnp.concatenate([jnp.arange(S), jnp.tile(PAST_LEN + jnp.arange(Ls), nb)])
    tab_a, tab_b = _tables(pos)

    x = jnp.concatenate([x_prompt.reshape(S, D_MODEL), x_sample.reshape(n_s, D_MODEL)], 0)
    xb = x.astype(BF16)
    cache_k = cache_a_k.reshape(depth, nb, win, A_KV_WIDTH)
    cache_v = cache_a_v.reshape(depth, nb, win, A_KV_WIDTH)
    vec3 = lambda a: a.reshape(depth, 1, a.shape[-1])
    b_in3, b_out3 = vec3(b_in), vec3(b_out)
    c_ln_g3, c_ln_b3 = vec3(c_ln_g), vec3(c_ln_b)
    ln1_g3, ln1_b3, ln2_g3, ln2_b3 = vec3(ln1_g), vec3(ln1_b), vec3(ln2_g), vec3(ln2_b)
    router_b3 = vec3(router_b)
    c_wbt = jnp.swapaxes(c_wb, 1, 2)
    exp_b_gu4 = exp_b_gu.reshape(depth, N_EXPERTS, 1, 2 * EXPERT_DIM)

    ak_p, av_p, rb_p, ak_s, av_s, rb_s, cv_s = [], [], [], [], [], [], []
    for l in range(depth):
        h = in_proj(xb, w_in, b_in3, l)
        oa_p, kr_p = attn_prompt(h, tab_a, a_sinks, l, S)
        oa_s, kr_s = attn_sample(h, tab_a, a_sinks, cache_k, cache_v, l, S, nb, Ls)
        ob_p, r_p = ret_prompt(h, tab_b, log_gamma, S)
        ob_s, r_s = ret_sample(h, tab_b, log_gamma, state_b, l, S, nb, Ls)
        (oc_p,) = cmlp(h, c_ln_g3, c_ln_b3, c_ws, c_wbt, l, 0, S // CMLP_CHUNK, CMLP_CHUNK, False)
        oc_s, vn_s = cmlp(h, c_ln_g3, c_ln_b3, c_ws, c_wbt, l, S, nb, Ls, True)
        oa = jnp.concatenate([oa_p, oa_s], 0)
        ob = jnp.concatenate([ob_p, ob_s], 0)
        oc = jnp.concatenate([oc_p, oc_s], 0)
        h2 = out_proj(oa, ob, oc, w_out, b_out3, l)
        x1, gates, idx, p, rank, counts = norm1_router(x, h2, ln1_g3, ln1_b3, router_w, router_b3, l, alpha)

        dest, pad_row, plan = routing_plan(idx, rank, counts)
        xs = dispatch(x1, dest, pad_row, plan[1])
        a = expert_up(xs, exp_w_gu, exp_b_gu4, plan, l)
        ys = expert_down(a, exp_w_d, plan, l)
        x2, x2b = combine_norm2(x1, ys, dest, p, gates, exp_b_d, ln2_g3, ln2_b3, l, alpha)

        p_l = jnp.concatenate([p_prompt[l].reshape(S, PLE_DIM), p_sample[l].reshape(n_s, PLE_DIM)], 0)
        x, xb = ple(x2, x2b, p_l, ple_gate_w, ple_w, l)

        ak_p.append(kr_p[S - win:].reshape(1, win, A_KV_HEADS, HEAD_DIM_A))
        av_p.append(h[S - win:S, OFF_AV:OFF_AV + A_KV_WIDTH].reshape(1, win, A_KV_HEADS, HEAD_DIM_A))
        rb_p.append(r_p[None])
        k_all = jnp.concatenate([cache_k[l], kr_s.reshape(nb, Ls, A_KV_WIDTH)], 1)
        v_all = jnp.concatenate([cache_v[l], h[S:, OFF_AV:OFF_AV + A_KV_WIDTH].reshape(nb, Ls, A_KV_WIDTH)], 1)
        ak_s.append(k_all[:, -win:].reshape(nb, win, A_KV_HEADS, HEAD_DIM_A))
        av_s.append(v_all[:, -win:].reshape(nb, win, A_KV_HEADS, HEAD_DIM_A))
        rb_s.append(r_s)
        cv_s.append(vn_s.reshape(nb, Ls, C_WIDTH))

    return (x[:S].reshape(1, S, D_MODEL), x[S:].reshape(nb, Ls, D_MODEL),
            jnp.stack(ak_p), jnp.stack(av_p), jnp.stack(rb_p),
            jnp.stack(ak_s), jnp.stack(av_s), jnp.stack(rb_s), jnp.stack(cv_s))
```

```python
import functools

import jax
import jax.numpy as jnp
from jax import lax
from jax.experimental import pallas as pl
from jax.experimental.pallas import tpu as pltpu

D_MODEL = 4096
CHUNK = 64
PAST_LEN = 4096
HEAD_DIM_A = 64
A_WIDTH = 1024
A_KV_HEADS = 4
A_GROUP = 4
A_KV_WIDTH = 256
ROT_DIM_A = 16
ROPE_THETA_A = 500000.0
B_HEAD_DIM = 256
B_WIDTH = 2048
B_HEADS = 8
RET_THETA = 10000.0
RET_BLOCK = 256
C_WIDTH = 1024
C_GROUPS = 4
C_GROUP_DIM = 256
CMLP_CHUNK = 128
OFF_AQ, OFF_AK, OFF_AV = 0, 1024, 1280
OFF_BQ, OFF_BK, OFF_BV, OFF_BG = 1536, 3584, 5632, 7680
OFF_CU, OFF_CV = 9728, 10752
IN_COLS = 11776
N_EXPERTS = 32
TOP_K = 4
EXPERT_DIM = 1024
SWIGLU_LIMIT = 7.0
SWIGLU_ALPHA = 1.702
PLE_DIM = 256
LN_EPS = 1e-5
RMS_EPS = 1e-6

LANES = 128
MOE_TILE = 256
VMEM_LIMIT = 56 * 1024 * 1024

BF16 = jnp.bfloat16
F32 = jnp.float32


def _params(*sem):
    return pltpu.CompilerParams(dimension_semantics=sem, vmem_limit_bytes=VMEM_LIMIT)


def _row_tile(n, target):
    best = None
    for t in range(16, min(n, target) + 1, 16):
        if n % t == 0:
            best = t
    assert best is not None, n
    return best


def _layer_norm(x, g, b):
    mu = jnp.mean(x, -1, keepdims=True)
    xc = x - mu
    var = jnp.mean(xc * xc, -1, keepdims=True)
    return xc * lax.rsqrt(var + LN_EPS) * g + b


def _gelu(x):
    c = 0.7978845608028654
    return 0.5 * x * (1.0 + jnp.tanh(c * (x + 0.044715 * (x * x * x))))


def _sigmoid(x):
    return 1.0 / (1.0 + jnp.exp(-x))


def _in_proj_kernel(x_ref, w_ref, b_ref, o_ref):
    w = w_ref[...].astype(BF16)
    o_ref[...] = jnp.dot(x_ref[...], w, preferred_element_type=F32) + b_ref[...]


def in_proj(xb, w_in, b_in3, l):
    T = xb.shape[0]
    n = w_in.shape[2]
    tm = _row_tile(T, 1088)
    tn = 512
    return pl.pallas_call(
        _in_proj_kernel,
        grid=(T // tm, n // tn),
        in_specs=[
            pl.BlockSpec((tm, D_MODEL), lambda i, j: (i, 0)),
            pl.BlockSpec((None, D_MODEL, tn), lambda i, j: (l, 0, j)),
            pl.BlockSpec((None, 1, tn), lambda i, j: (l, 0, j)),
        ],
        out_specs=pl.BlockSpec((tm, tn), lambda i, j: (i, j)),
        out_shape=jax.ShapeDtypeStruct((T, n), F32),
        compiler_params=_params("parallel", "parallel"),
        name="in_proj",
    )(xb, w_in, b_in3)


def _out_proj_kernel(oa_ref, ob_ref, oc_ref, w_ref, b_ref, o_ref):
    w = w_ref[...].astype(BF16)
    acc = jnp.dot(oa_ref[...], w[:A_WIDTH], preferred_element_type=F32)
    acc += jnp.dot(ob_ref[...], w[A_WIDTH:A_WIDTH + B_WIDTH], preferred_element_type=F32)
    acc += jnp.dot(oc_ref[...], w[A_WIDTH + B_WIDTH:], preferred_element_type=F32)
    o_ref[...] = acc + b_ref[...]


def out_proj(oa, ob, oc, w_out, b_out3, l):
    T = oa.shape[0]
    tm = _row_tile(T, 1088)
    tn = 512
    return pl.pallas_call(
        _out_proj_kernel,
        grid=(T // tm, D_MODEL // tn),
        in_specs=[
            pl.BlockSpec((tm, A_WIDTH), lambda i, j: (i, 0)),
            pl.BlockSpec((tm, B_WIDTH), lambda i, j: (i, 0)),
            pl.BlockSpec((tm, C_WIDTH), lambda i, j: (i, 0)),
            pl.BlockSpec((None, D_MODEL, tn), lambda i, j: (l, 0, j)),
            pl.BlockSpec((None, 1, tn), lambda i, j: (l, 0, j)),
        ],
        out_specs=pl.BlockSpec((tm, tn), lambda i, j: (i, j)),
        out_shape=jax.ShapeDtypeStruct((T, D_MODEL), F32),
        compiler_params=_params("parallel", "parallel"),
        name="out_proj",
    )(oa, ob, oc, w_out, b_out3)


def _rot_a(x, tab):
    c, s_lo, s_hi = tab[:, :LANES], tab[:, LANES:2 * LANES], tab[:, 2 * LANES:]
    half = ROT_DIM_A // 2
    outs = []
    for j in range(x.shape[1] // LANES):
        xc = x[:, j * LANES:(j + 1) * LANES]
        up = pltpu.roll(xc, LANES - half, 1)
        dn = pltpu.roll(xc, half, 1)
        outs.append(xc * c + up * s_lo + dn * s_hi)
    return outs[0] if len(outs) == 1 else jnp.concatenate(outs, axis=1)


def _attend(q, k, v, valid, sink_ref, l, o_ref, acc_ref):
    R = q.shape[0]
    qb = q.astype(BF16)
    kb = k.astype(BF16)
    vb = v.astype(BF16)
    grp = lax.broadcasted_iota(jnp.int32, (A_GROUP * R, 1), 0) // R
    for kh in range(A_KV_HEADS):
        base = kh * A_GROUP
        qs = jnp.concatenate(
            [qb[:, (base + g) * HEAD_DIM_A:(base + g + 1) * HEAD_DIM_A] for g in range(A_GROUP)], axis=0)
        k_h = kb[:, kh * HEAD_DIM_A:(kh + 1) * HEAD_DIM_A]
        v_h = vb[:, kh * HEAD_DIM_A:(kh + 1) * HEAD_DIM_A]
        s = lax.dot_general(qs, k_h, (((1,), (1,)), ((), ())), preferred_element_type=F32)
        if valid is not None:
            s = jnp.where(valid, s, -jnp.inf)
        sink = jnp.zeros((A_GROUP * R, 1), F32)
        for g in range(A_GROUP):
            sink = jnp.where(grp == g, sink_ref[l, base + g], sink)
        m = jnp.maximum(jnp.max(s, -1, keepdims=True), sink)
        e = jnp.exp(s - m)
        p = e / (jnp.sum(e, -1, keepdims=True) + jnp.exp(sink - m))
        o = jnp.dot(p.astype(BF16), v_h, preferred_element_type=F32)
        for g in range(A_GROUP):
            h = base + g
            acc_ref[:, h * HEAD_DIM_A:(h + 1) * HEAD_DIM_A] = o[g * R:(g + 1) * R]
    o_ref[...] = acc_ref[...].astype(o_ref.dtype)


def _attn_prompt_kernel(sink_ref, q_ref, kp_ref, kc_ref, vp_ref, vc_ref, tq_ref, tp_ref, o_ref, kr_ref, acc_ref,
                        *, l):
    i = pl.program_id(0)
    R = q_ref.shape[0]
    tq = tq_ref[...]
    q = _rot_a(q_ref[...], tq) * (HEAD_DIM_A ** -0.5)
    kc = _rot_a(kc_ref[...], tq)
    kp = _rot_a(kp_ref[...], tp_ref[...])
    kr_ref[...] = kc
    k = jnp.concatenate([kp, kc], axis=0)
    v = jnp.concatenate([vp_ref[...], vc_ref[...]], axis=0)
    n_q = A_GROUP * R
    qc = (lax.broadcasted_iota(jnp.int32, (n_q, 2 * R), 0) % R) // CHUNK
    col = lax.broadcasted_iota(jnp.int32, (n_q, 2 * R), 1)
    kc_idx = col // CHUNK - R // CHUNK
    lo = jnp.where(i == 0, 0, -(R // CHUNK))
    valid = (kc_idx <= qc) & (kc_idx >= jnp.maximum(qc - 2, lo))
    _attend(q, k, v, valid, sink_ref, l, o_ref, acc_ref)


def attn_prompt(h, tab_a, sinks, l, S):
    R = 2 * CHUNK
    nb = S // R
    prev = lambda i: jnp.maximum(i - 1, 0)
    return pl.pallas_call(
        functools.partial(_attn_prompt_kernel, l=l),
        grid=(nb,),
        in_specs=[
            pl.BlockSpec(memory_space=pltpu.SMEM),
            pl.BlockSpec((R, A_WIDTH), lambda i: (i, 0)),
            pl.BlockSpec((R, A_KV_WIDTH), lambda i: (prev(i), OFF_AK // A_KV_WIDTH)),
            pl.BlockSpec((R, A_KV_WIDTH), lambda i: (i, OFF_AK // A_KV_WIDTH)),
            pl.BlockSpec((R, A_KV_WIDTH), lambda i: (prev(i), OFF_AV // A_KV_WIDTH)),
            pl.BlockSpec((R, A_KV_WIDTH), lambda i: (i, OFF_AV // A_KV_WIDTH)),
            pl.BlockSpec((R, 3 * LANES), lambda i: (i, 0)),
            pl.BlockSpec((R, 3 * LANES), lambda i: (prev(i), 0)),
        ],
        out_specs=[
            pl.BlockSpec((R, A_WIDTH), lambda i: (i, 0)),
            pl.BlockSpec((R, A_KV_WIDTH), lambda i: (i, 0)),
        ],
        out_shape=[
            jax.ShapeDtypeStruct((S, A_WIDTH), BF16),
            jax.ShapeDtypeStruct((S, A_KV_WIDTH), F32),
        ],
        scratch_shapes=[pltpu.VMEM((R, A_WIDTH), F32)],
        compiler_params=_params("parallel"),
        name="attn_prompt",
    )(sinks, h, h, h, h, h, tab_a, tab_a)


def _attn_sample_kernel(sink_ref, q_ref, kn_ref, vn_ref, kcache_ref, vcache_ref, tq_ref, o_ref, kr_ref, acc_ref,
                        *, l):
    tq = tq_ref[...]
    q = _rot_a(q_ref[...], tq) * (HEAD_DIM_A ** -0.5)
    kn = _rot_a(kn_ref[...], tq)
    kr_ref[...] = kn
    k = jnp.concatenate([kcache_ref[...], kn], axis=0)
    v = jnp.concatenate([vcache_ref[...], vn_ref[...]], axis=0)
    _attend(q, k, v, None, sink_ref, l, o_ref, acc_ref)


def attn_sample(h, tab_a, sinks, cache_k, cache_v, l, S, nb, Ls):
    win = cache_k.shape[2]
    r0 = S // Ls
    return pl.pallas_call(
        functools.partial(_attn_sample_kernel, l=l),
        grid=(nb,),
        in_specs=[
            pl.BlockSpec(memory_space=pltpu.SMEM),
            pl.BlockSpec((Ls, A_WIDTH), lambda b: (r0 + b, 0)),
            pl.BlockSpec((Ls, A_KV_WIDTH), lambda b: (r0 + b, OFF_AK // A_KV_WIDTH)),
            pl.BlockSpec((Ls, A_KV_WIDTH), lambda b: (r0 + b, OFF_AV // A_KV_WIDTH)),
            pl.BlockSpec((None, None, win, A_KV_WIDTH), lambda b: (l, b, 0, 0)),
            pl.BlockSpec((None, None, win, A_KV_WIDTH), lambda b: (l, b, 0, 0)),
            pl.BlockSpec((Ls, 3 * LANES), lambda b: (r0 + b, 0)),
        ],
        out_specs=[
            pl.BlockSpec((Ls, A_WIDTH), lambda b: (b, 0)),
            pl.BlockSpec((Ls, A_KV_WIDTH), lambda b: (b, 0)),
        ],
        out_shape=[
            jax.ShapeDtypeStruct((nb * Ls, A_WIDTH), BF16),
            jax.ShapeDtypeStruct((nb * Ls, A_KV_WIDTH), F32),
        ],
        scratch_shapes=[pltpu.VMEM((Ls, A_WIDTH), F32)],
        compiler_params=_params("parallel"),
        name="attn_sample",
    )(sinks, h, h, h, cache_k, cache_v, tab_a)


def _rot_b(x, cs):
    half = B_HEAD_DIM // 2
    c, s = cs[:, :half], cs[:, half:]
    x1, x2 = x[:, :half], x[:, half:]
    return jnp.concatenate([x1 * c - x2 * s, x2 * c + x1 * s], axis=1)


def _retention_step(lg, q, k, v, g, cs, r):
    L = q.shape[0]
    q = _rot_b(q, cs)
    k = _rot_b(k, cs) * (B_HEAD_DIM ** -0.5)
    ii = lax.broadcasted_iota(jnp.int32, (L, L), 0)
    jj = lax.broadcasted_iota(jnp.int32, (L, L), 1)
    diff = (ii - jj).astype(F32)
    decay = jnp.where(diff >= 0, jnp.exp(lg * jnp.maximum(diff, 0.0)), 0.0)
    row = lax.broadcasted_iota(jnp.int32, (L, B_HEAD_DIM), 0).astype(F32)
    qb = q.astype(BF16)
    vb = v.astype(BF16)
    scores = lax.dot_general(qb, k.astype(BF16), (((1,), (1,)), ((), ())), preferred_element_type=F32) * decay
    inner = jnp.dot(scores.astype(BF16), vb, preferred_element_type=F32)
    cross = jnp.dot(qb, r.astype(BF16), preferred_element_type=F32) * jnp.exp(lg * (row + 1.0))
    k_dec = k * jnp.exp(lg * (L - 1.0 - row))
    g_all = jnp.exp(lg * jnp.full((1, B_HEAD_DIM), float(L), F32))
    r_new = r * g_all + jnp.dot(k_dec.T.astype(BF16), vb, preferred_element_type=F32)
    o = inner + cross
    on = o * lax.rsqrt(jnp.mean(o * o, -1, keepdims=True) + RMS_EPS)
    return on * (g * _sigmoid(g)), r_new


def _ret_prompt_kernel(lg_ref, q_ref, k_ref, v_ref, g_ref, cs_ref, o_ref, r_ref, r_scr):
    hd = pl.program_id(0)
    c = pl.program_id(1)

    @pl.when(c == 0)
    def _():
        r_scr[...] = jnp.zeros_like(r_scr)

    o, r_new = _retention_step(lg_ref[hd], q_ref[...], k_ref[...], v_ref[...], g_ref[...], cs_ref[...], r_scr[...])
    o_ref[...] = o.astype(o_ref.dtype)
    r_scr[...] = r_new
    r_ref[...] = r_new


def ret_prompt(h, tab_b, log_gamma, S):
    L = RET_BLOCK
    nc = S // L
    col = lambda off: (lambda hd, c: (c, off // B_HEAD_DIM + hd))
    return pl.pallas_call(
        _ret_prompt_kernel,
        grid=(B_HEADS, nc),
        in_specs=[
            pl.BlockSpec(memory_space=pltpu.SMEM),
            pl.BlockSpec((L, B_HEAD_DIM), col(OFF_BQ)),
            pl.BlockSpec((L, B_HEAD_DIM), col(OFF_BK)),
            pl.BlockSpec((L, B_HEAD_DIM), col(OFF_BV)),
            pl.BlockSpec((L, B_HEAD_DIM), col(OFF_BG)),
            pl.BlockSpec((L, B_HEAD_DIM), lambda hd, c: (c, 0)),
        ],
        out_specs=[
            pl.BlockSpec((L, B_HEAD_DIM), lambda hd, c: (c, hd)),
            pl.BlockSpec((None, B_HEAD_DIM, B_HEAD_DIM), lambda hd, c: (hd, 0, 0)),
        ],
        out_shape=[
            jax.ShapeDtypeStruct((S, B_WIDTH), BF16),
            jax.ShapeDtypeStruct((B_HEADS, B_HEAD_DIM, B_HEAD_DIM), F32),
        ],
        scratch_shapes=[pltpu.VMEM((B_HEAD_DIM, B_HEAD_DIM), F32)],
        compiler_params=_params("parallel", "arbitrary"),
        name="ret_prompt",
    )(log_gamma, h, h, h, h, tab_b)


def _ret_sample_kernel(lg_ref, q_ref, k_ref, v_ref, g_ref, cs_ref, r0_ref, o_ref, r_ref):
    hd = pl.program_id(1)
    o, r_new = _retention_step(lg_ref[hd], q_ref[...], k_ref[...], v_ref[...], g_ref[...], cs_ref[...], r0_ref[...])
    o_ref[...] = o.astype(o_ref.dtype)
    r_ref[...] = r_new


def ret_sample(h, tab_b, log_gamma, state_b, l, S, nb, Ls):
    r0 = S // Ls
    col = lambda off: (lambda b, hd: (r0 + b, off // B_HEAD_DIM + hd))
    return pl.pallas_call(
        _ret_sample_kernel,
        grid=(nb, B_HEADS),
        in_specs=[
            pl.BlockSpec(memory_space=pltpu.SMEM),
            pl.BlockSpec((Ls, B_HEAD_DIM), col(OFF_BQ)),
            pl.BlockSpec((Ls, B_HEAD_DIM), col(OFF_BK)),
            pl.BlockSpec((Ls, B_HEAD_DIM), col(OFF_BV)),
            pl.BlockSpec((Ls, B_HEAD_DIM), col(OFF_BG)),
            pl.BlockSpec((Ls, B_HEAD_DIM), lambda b, hd: (r0 + b, 0)),
            pl.BlockSpec((None, None, None, B_HEAD_DIM, B_HEAD_DIM), lambda b, hd: (l, b, hd, 0, 0)),
        ],
        out_specs=[
            pl.BlockSpec((Ls, B_HEAD_DIM), lambda b, hd: (b, hd)),
            pl.BlockSpec((None, None, B_HEAD_DIM, B_HEAD_DIM), lambda b, hd: (b, hd, 0, 0)),
        ],
        out_shape=[
            jax.ShapeDtypeStruct((nb * Ls, B_WIDTH), BF16),
            jax.ShapeDtypeStruct((nb, B_HEADS, B_HEAD_DIM, B_HEAD_DIM), F32),
        ],
        compiler_params=_params("parallel", "parallel"),
        name="ret_sample",
    )(log_gamma, h, h, h, h, tab_b, state_b)


def _cmlp_kernel(u0_ref, u1_ref, v0_ref, v1_ref, g_ref, b_ref, ws_ref, wbt_ref, o_ref, *maybe_vn_ref):
    R = u0_ref.shape[0]
    u = _gelu(jnp.concatenate([u0_ref[...], u1_ref[...]], axis=1))
    v = jnp.concatenate([v0_ref[...], v1_ref[...]], axis=1)
    vn = _layer_norm(_gelu(v), g_ref[...], b_ref[...])
    if maybe_vn_ref:
        maybe_vn_ref[0][...] = vn
    vb = vn.astype(BF16)
    ii = lax.broadcasted_iota(jnp.int32, (R, R), 0) // CHUNK
    jj = lax.broadcasted_iota(jnp.int32, (R, R), 1) // CHUNK
    for gi in range(C_GROUPS):
        wm = jnp.where(jj <= ii, ws_ref[gi][:R, :R], 0.0).astype(BF16)
        sl = slice(gi * C_GROUP_DIM, (gi + 1) * C_GROUP_DIM)
        sv = jnp.dot(wm, vb[:, sl], preferred_element_type=F32) + wbt_ref[:R, gi:gi + 1]
        o_ref[:, sl] = (u[:, sl] * sv).astype(o_ref.dtype)


def cmlp(h, c_ln_g3, c_ln_b3, c_ws, c_wbt, l, row0, n_blocks, R, with_vn):
    b0 = row0 // R
    hw = C_WIDTH // 2
    out_shape = [jax.ShapeDtypeStruct((n_blocks * R, C_WIDTH), BF16)]
    out_specs = [pl.BlockSpec((R, C_WIDTH), lambda i: (i, 0))]
    if with_vn:
        out_shape.append(jax.ShapeDtypeStruct((n_blocks * R, C_WIDTH), F32))
        out_specs.append(pl.BlockSpec((R, C_WIDTH), lambda i: (i, 0)))
    return pl.pallas_call(
        _cmlp_kernel,
        grid=(n_blocks,),
        in_specs=[
            pl.BlockSpec((R, hw), lambda i: (b0 + i, OFF_CU // hw)),
            pl.BlockSpec((R, hw), lambda i: (b0 + i, OFF_CU // hw + 1)),
            pl.BlockSpec((R, hw), lambda i: (b0 + i, OFF_CV // hw)),
            pl.BlockSpec((R, hw), lambda i: (b0 + i, OFF_CV // hw + 1)),
            pl.BlockSpec((None, 1, C_WIDTH), lambda i: (l, 0, 0)),
            pl.BlockSpec((None, 1, C_WIDTH), lambda i: (l, 0, 0)),
            pl.BlockSpec((None, C_GROUPS, CMLP_CHUNK, CMLP_CHUNK), lambda i: (l, 0, 0, 0)),
            pl.BlockSpec((None, CMLP_CHUNK, C_GROUPS), lambda i: (l, 0, 0)),
        ],
        out_specs=out_specs,
        out_shape=out_shape,
        compiler_params=_params("parallel"),
        name="cmlp_vn" if with_vn else "cmlp",
    )(h, h, h, h, c_ln_g3, c_ln_b3, c_ws, c_wbt)


def _split_bf16(x):
    hi = x.astype(BF16)
    lo = (x - hi.astype(F32)).astype(BF16)
    return hi, lo


def _norm1_router_kernel(x_ref, h_ref, g_ref, b_ref, rw_ref, rb_ref,
                         x1_ref, gates_ref, idx_ref, p_ref, rank_ref, counts_ref, cnt_scr, *, alpha):
    @pl.when(pl.program_id(0) == 0)
    def _():
        cnt_scr[...] = jnp.zeros_like(cnt_scr)

    x1 = _layer_norm(alpha * x_ref[...] + h_ref[...], g_ref[...], b_ref[...])
    x1_ref[...] = x1
    xh, xl = _split_bf16(x1)
    wh, wl = _split_bf16(rw_ref[...])
    dot = functools.partial(jnp.dot, preferred_element_type=F32)
    logits = dot(xh, wh) + (dot(xh, wl) + dot(xl, wh)) + rb_ref[...]
    R = logits.shape[0]
    lane = lax.broadcasted_iota(jnp.int32, (R, N_EXPERTS), 1).astype(F32)
    lane_k = lax.broadcasted_iota(jnp.int32, (R, TOP_K), 1)
    work = logits
    vals, idxs = [], []
    for _ in range(TOP_K):
        m = jnp.max(work, -1, keepdims=True)
        ix = jnp.min(jnp.where(work == m, lane, float(N_EXPERTS)), -1, keepdims=True)
        vals.append(m)
        idxs.append(ix)
        work = jnp.where(lane == ix, -jnp.inf, work)
    es = [jnp.exp(v - vals[0]) for v in vals]
    tot = es[0] + es[1] + es[2] + es[3]
    gates = jnp.zeros((R, N_EXPERTS), F32)
    idx_out = jnp.zeros((R, TOP_K), jnp.int32)
    p_out = jnp.zeros((R, TOP_K), F32)
    for k in range(TOP_K):
        pk = es[k] / tot
        gates = jnp.where(lane == idxs[k], pk, gates)
        idx_out = jnp.where(lane_k == k, idxs[k].astype(jnp.int32), idx_out)
        p_out = jnp.where(lane_k == k, pk, p_out)
    gates_ref[...] = gates
    idx_ref[...] = idx_out
    p_ref[...] = p_out
    onehot = jnp.zeros((R, N_EXPERTS), F32)
    for k in range(TOP_K):
        onehot = jnp.where(lane == idxs[k], 1.0, onehot)
    before = lax.broadcasted_iota(jnp.int32, (R, R), 0) > lax.broadcasted_iota(jnp.int32, (R, R), 1)
    prefix = jnp.dot(jnp.where(before, 1.0, 0.0).astype(BF16), onehot.astype(BF16), preferred_element_type=F32)
    base = cnt_scr[...] + prefix
    rank_out = jnp.zeros((R, TOP_K), jnp.int32)
    for k in range(TOP_K):
        rk = jnp.sum(jnp.where(lane == idxs[k], base, 0.0), -1, keepdims=True)
        rank_out = jnp.where(lane_k == k, rk.astype(jnp.int32), rank_out)
    rank_ref[...] = rank_out
    cnt_scr[...] += jnp.sum(onehot, 0, keepdims=True)
    counts_ref[...] = cnt_scr[...]


def norm1_router(x, h2, ln_g3, ln_b3, router_w, router_b3, l, alpha):
    T = x.shape[0]
    tm = _row_tile(T, 256)
    row = lambda i: (i, 0)
    vec = lambda i: (l, 0, 0)
    return pl.pallas_call(
        functools.partial(_norm1_router_kernel, alpha=alpha),
        grid=(T // tm,),
        in_specs=[
            pl.BlockSpec((tm, D_MODEL), row),
            pl.BlockSpec((tm, D_MODEL), row),
            pl.BlockSpec((None, 1, D_MODEL), vec),
            pl.BlockSpec((None, 1, D_MODEL), vec),
            pl.BlockSpec((None, D_MODEL, N_EXPERTS), vec),
            pl.BlockSpec((None, 1, N_EXPERTS), vec),
        ],
        out_specs=[
            pl.BlockSpec((tm, D_MODEL), row),
            pl.BlockSpec((tm, N_EXPERTS), row),
            pl.BlockSpec((tm, TOP_K), row),
            pl.BlockSpec((tm, TOP_K), row),
            pl.BlockSpec((tm, TOP_K), row),
            pl.BlockSpec((1, N_EXPERTS), lambda i: (0, 0)),
        ],
        out_shape=[
            jax.ShapeDtypeStruct((T, D_MODEL), F32),
            jax.ShapeDtypeStruct((T, N_EXPERTS), F32),
            jax.ShapeDtypeStruct((T, TOP_K), jnp.int32),
            jax.ShapeDtypeStruct((T, TOP_K), F32),
            jax.ShapeDtypeStruct((T, TOP_K), jnp.int32),
            jax.ShapeDtypeStruct((1, N_EXPERTS), F32),
        ],
        scratch_shapes=[pltpu.VMEM((1, N_EXPERTS), F32)],
        compiler_params=_params("arbitrary"),
        name="norm1_router",
    )(x, h2, ln_g3, ln_b3, router_w, router_b3)


def routing_plan(idx, rank, counts):
    T = idx.shape[0]
    n_tiles = T * TOP_K // MOE_TILE + N_EXPERTS
    counts = counts.reshape(N_EXPERTS).astype(jnp.int32)
    padded = ((counts + MOE_TILE - 1) // MOE_TILE) * MOE_TILE
    pend = jnp.cumsum(padded)
    pstart = pend - padded
    experts = jnp.arange(N_EXPERTS, dtype=jnp.int32)
    dest = jnp.sum(jnp.where(idx[:, :, None] == experts, pstart, 0), -1) + rank
    tile_start = jnp.arange(n_tiles, dtype=jnp.int32) * MOE_TILE
    tile_e = jnp.minimum(jnp.sum((pend[None, :] <= tile_start[:, None]).astype(jnp.int32), 1), N_EXPERTS - 1)
    n_used = pend[-1] // MOE_TILE
    pad_row = jnp.where(padded > 0, pend - MOE_TILE, -1).astype(jnp.int32)
    tile = jnp.arange(n_tiles, dtype=jnp.int32)
    prev_e = jnp.concatenate([jnp.full((1,), -1, jnp.int32), tile_e[:-1]])
    first = ((tile < n_used) & (tile_e != prev_e)).astype(jnp.int32)
    run_end = jnp.sum(jnp.where(tile_e[:, None] == experts, pend // MOE_TILE, 0), -1)
    wraps = (run_end >= n_used).astype(jnp.int32)
    nxt_tile = jnp.where(wraps == 1, 0, jnp.minimum(run_end, n_tiles - 1))
    nxt_e = jnp.sum(jnp.where(nxt_tile[:, None] == tile[None, :], tile_e[None, :], 0), -1)
    plan = (tile_e, n_used.astype(jnp.int32).reshape(1), first, nxt_e.astype(jnp.int32), wraps)
    return dest, pad_row, plan


def _dma_rows(n_rows, body):
    unroll = 8
    assert n_rows % unroll == 0

    def trip(t, carry):
        for u in range(unroll):
            body(t * unroll + u)
        return carry

    lax.fori_loop(0, n_rows // unroll, trip, 0)


def _pack_bf16_pairs(x):
    c = x.shape[1] // 2
    lo = pltpu.bitcast(x[:, :c].astype(BF16).astype(F32), jnp.uint32)
    hi = pltpu.bitcast(x[:, c:].astype(BF16).astype(F32), jnp.uint32)
    return (lo >> 16) | (hi & jnp.uint32(0xFFFF0000))


def _unpack_bf16_pairs(p):
    lo = pltpu.bitcast(p << 16, F32)
    hi = pltpu.bitcast(p & jnp.uint32(0xFFFF0000), F32)
    return jnp.concatenate([lo, hi], axis=1)


def _dispatch_kernel(pad_ref, nu_ref, dest_ref, x_ref, xs_hbm, zeros_buf, packed_buf, zero_sem, row_sem):
    tm = x_ref.shape[0]
    n_tiles = xs_hbm.shape[0] // MOE_TILE

    def zero_copy(row):
        start = pl.multiple_of(row, MOE_TILE)
        return pltpu.make_async_copy(zeros_buf, xs_hbm.at[pl.ds(start, MOE_TILE), :], zero_sem)

    @pl.when(pl.program_id(0) == 0)
    def _():
        zeros_buf[...] = jnp.zeros_like(zeros_buf)
        for e in range(N_EXPERTS):
            @pl.when(pad_ref[e] >= 0)
            def _():
                zero_copy(pad_ref[e]).start()

        def start_tail(t, c):
            zero_copy(t * MOE_TILE).start()
            return c

        def wait_tail(t, c):
            zero_copy(t * MOE_TILE).wait()
            return c

        lax.fori_loop(nu_ref[0], n_tiles, start_tail, 0)
        for e in range(N_EXPERTS):
            @pl.when(pad_ref[e] >= 0)
            def _():
                zero_copy(pad_ref[e]).wait()
        lax.fori_loop(nu_ref[0], n_tiles, wait_tail, 0)

    i = pl.program_id(0)
    slot = i % 2
    packed_buf[slot] = _pack_bf16_pairs(x_ref[...])

    def start_row(r):
        for k in range(TOP_K):
            d = dest_ref[0, r * TOP_K + k]
            pltpu.make_async_copy(packed_buf.at[slot, pl.ds(r, 1), :], xs_hbm.at[pl.ds(d, 1), :],
                                  row_sem.at[slot]).start()

    _dma_rows(tm, start_row)

    def wait_rows(s):
        for k in range(TOP_K):
            pltpu.make_async_copy(packed_buf.at[s], xs_hbm.at[pl.ds(0, tm), :], row_sem.at[s]).wait()

    @pl.when(i > 0)
    def _():
        wait_rows(1 - slot)

    @pl.when(i == pl.num_programs(0) - 1)
    def _():
        wait_rows(slot)


def dispatch(x1, dest, pad_row, n_used):
    T = x1.shape[0]
    tm = _row_tile(T, 256)
    n_rows = (T * TOP_K // MOE_TILE + N_EXPERTS) * MOE_TILE
    dest3 = dest.reshape(T // tm, 1, tm * TOP_K)
    return pl.pallas_call(
        _dispatch_kernel,
        grid_spec=pltpu.PrefetchScalarGridSpec(
            num_scalar_prefetch=2,
            grid=(T // tm,),
            in_specs=[
                pl.BlockSpec((None, 1, tm * TOP_K), lambda i, *_: (i, 0, 0), memory_space=pltpu.SMEM),
                pl.BlockSpec((tm, D_MODEL), lambda i, *_: (i, 0)),
            ],
            out_specs=pl.BlockSpec(memory_space=pl.ANY),
            scratch_shapes=[
                pltpu.VMEM((MOE_TILE, D_MODEL // 2), jnp.uint32),
                pltpu.VMEM((2, tm, D_MODEL // 2), jnp.uint32),
                pltpu.SemaphoreType.DMA(()),
                pltpu.SemaphoreType.DMA((2,)),
            ],
        ),
        out_shape=jax.ShapeDtypeStruct((n_rows, D_MODEL // 2), jnp.uint32),
        compiler_params=_params("arbitrary"),
        name="dispatch",
    )(pad_row, n_used, dest3, x1)


def _switch_weights(first, very_first, own_e, own_pass, nxt_e, nxt_pass, has_next, slot_ref, copies):
    @pl.when(very_first)
    def _():
        slot_ref[0] = 1
        for c in copies(own_e, own_pass, 0):
            c.start()

    @pl.when(first)
    def _():
        slot = 1 - slot_ref[0]
        slot_ref[0] = slot
        for c in copies(own_e, own_pass, slot):
            c.wait()

        @pl.when(has_next)
        def _():
            for c in copies(nxt_e, nxt_pass, 1 - slot):
                c.start()


def _expert_up_kernel(te_ref, nu_ref, first_ref, nxt_ref, wraps_ref, x_ref, wgu_hbm, bg_ref, bu_ref, a_ref,
                      wbuf, wsem, slot_ref, *, l, tn):
    j = pl.program_id(0)
    m = pl.program_id(1)
    nj = pl.num_programs(0)

    def copies(e, jj, slot):
        cols = lambda part: pl.ds(pl.multiple_of(part * EXPERT_DIM + jj * tn, tn), tn)
        return [pltpu.make_async_copy(wgu_hbm.at[l, e, :, cols(part)], wbuf.at[slot, part], wsem.at[slot, part])
                for part in range(2)]

    @pl.when(m < nu_ref[0])
    def _():
        _switch_weights(first_ref[m] == 1, (j == 0) & (m == 0), te_ref[m], j,
                        nxt_ref[m], j + wraps_ref[m], j + wraps_ref[m] < nj, slot_ref, copies)
        slot = slot_ref[0]
        x = _unpack_bf16_pairs(x_ref[...]).astype(BF16)
        hg = jnp.dot(x, wbuf[slot, 0].astype(BF16), preferred_element_type=F32) + bg_ref[...]
        hu = jnp.dot(x, wbuf[slot, 1].astype(BF16), preferred_element_type=F32) + bu_ref[...]
        g = jnp.minimum(hg, SWIGLU_LIMIT)
        u = jnp.clip(hu, -SWIGLU_LIMIT, SWIGLU_LIMIT)
        a_ref[...] = ((u + 1.0) * g * _sigmoid(SWIGLU_ALPHA * g)).astype(a_ref.dtype)

    @pl.when(m >= nu_ref[0])
    def _():
        a_ref[...] = jnp.zeros_like(a_ref)


def expert_up(xs, exp_w_gu, exp_b_gu4, plan, l):
    tile_e, n_used, first, nxt_e, wraps = plan
    rows = xs.shape[0]
    n_tiles = rows // MOE_TILE
    tn = 512
    nj = EXPERT_DIM // tn
    mc = lambda m, nu: jnp.minimum(m, nu[0] - 1)
    return pl.pallas_call(
        functools.partial(_expert_up_kernel, l=l, tn=tn),
        grid_spec=pltpu.PrefetchScalarGridSpec(
            num_scalar_prefetch=5,
            grid=(nj, n_tiles),
            in_specs=[
                pl.BlockSpec((MOE_TILE, D_MODEL // 2), lambda j, m, te, nu, *_: (mc(m, nu), 0)),
                pl.BlockSpec(memory_space=pl.ANY),
                pl.BlockSpec((None, None, 1, tn), lambda j, m, te, nu, *_: (l, te[mc(m, nu)], 0, j)),
                pl.BlockSpec((None, None, 1, tn), lambda j, m, te, nu, *_: (l, te[mc(m, nu)], 0, nj + j)),
            ],
            out_specs=pl.BlockSpec((MOE_TILE, tn), lambda j, m, *_: (m, j)),
            scratch_shapes=[
                pltpu.VMEM((2, 2, D_MODEL, tn), F32),
                pltpu.SemaphoreType.DMA((2, 2)),
                pltpu.SMEM((1,), jnp.int32),
            ],
        ),
        out_shape=jax.ShapeDtypeStruct((rows, EXPERT_DIM), BF16),
        compiler_params=_params("arbitrary", "arbitrary"),
        name="expert_up",
    )(tile_e, n_used, first, nxt_e, wraps, xs, exp_w_gu, exp_b_gu4, exp_b_gu4)


def _expert_down_kernel(te_ref, nu_ref, first_ref, nxt_ref, wraps_ref, a_ref, wd_hbm, y_ref,
                        wbuf, wsem, slot_ref, *, l):
    m = pl.program_id(0)

    def copies(e, _, slot):
        return [pltpu.make_async_copy(wd_hbm.at[l, e], wbuf.at[slot], wsem.at[slot])]

    @pl.when(m < nu_ref[0])
    def _():
        _switch_weights(first_ref[m] == 1, m == 0, te_ref[m], 0, nxt_ref[m], 0, wraps_ref[m] == 0,
                        slot_ref, copies)
        y = jnp.dot(a_ref[...], wbuf[slot_ref[0]].astype(BF16), preferred_element_type=F32)
        y_ref[...] = _pack_bf16_pairs(y)

    @pl.when(m >= nu_ref[0])
    def _():
        y_ref[...] = jnp.zeros_like(y_ref)


def expert_down(a, exp_w_d, plan, l):
    tile_e, n_used, first, nxt_e, wraps = plan
    rows = a.shape[0]
    n_tiles = rows // MOE_TILE
    mc = lambda m, nu: jnp.minimum(m, nu[0] - 1)
    return pl.pallas_call(
        functools.partial(_expert_down_kernel, l=l),
        grid_spec=pltpu.PrefetchScalarGridSpec(
            num_scalar_prefetch=5,
            grid=(n_tiles,),
            in_specs=[
                pl.BlockSpec((MOE_TILE, EXPERT_DIM), lambda m, te, nu, *_: (mc(m, nu), 0)),
                pl.BlockSpec(memory_space=pl.ANY),
            ],
            out_specs=pl.BlockSpec((MOE_TILE, D_MODEL // 2), lambda m, *_: (m, 0)),
            scratch_shapes=[
                pltpu.VMEM((2, EXPERT_DIM, D_MODEL), F32),
                pltpu.SemaphoreType.DMA((2,)),
                pltpu.SMEM((1,), jnp.int32),
            ],
        ),
        out_shape=jax.ShapeDtypeStruct((rows, D_MODEL // 2), jnp.uint32),
        compiler_params=_params("arbitrary"),
        name="expert_down",
    )(tile_e, n_used, first, nxt_e, wraps, a, exp_w_d)


def _combine_norm2_kernel(dcur_ref, dnext_ref, x1_ref, ys_hbm, p_ref, gates_ref, bd_ref, g_ref, b_ref,
                          x2_ref, x2b_ref, rows_buf, sems, *, alpha):
    i = pl.program_id(0)
    n = pl.num_programs(0)
    tm = x1_ref.shape[0]
    slot = i % 2

    def gather(d_ref, s):
        def start_row(r):
            pltpu.make_async_copy(ys_hbm.at[pl.ds(d_ref[0, r], 1), :], rows_buf.at[s, pl.ds(r, 1), :],
                                  sems.at[s]).start()
        _dma_rows(TOP_K * tm, start_row)

    @pl.when(i == 0)
    def _():
        gather(dcur_ref, 0)

    @pl.when(i + 1 < n)
    def _():
        gather(dnext_ref, 1 - slot)

    pltpu.make_async_copy(ys_hbm.at[pl.ds(0, TOP_K * tm), :], rows_buf.at[slot], sems.at[slot]).wait()
    p = p_ref[...]
    y = jnp.dot(gates_ref[...].astype(BF16), bd_ref[...].astype(BF16), preferred_element_type=F32)
    for k in range(TOP_K):
        y += p[:, k:k + 1] * _unpack_bf16_pairs(rows_buf[slot, k * tm:(k + 1) * tm, :])
    x2 = _layer_norm(alpha * x1_ref[...] + y, g_ref[...], b_ref[...])
    x2_ref[...] = x2
    x2b_ref[...] = x2.astype(BF16)


def combine_norm2(x1, ys, dest, p, gates, exp_b_d, ln_g3, ln_b3, l, alpha):
    T = x1.shape[0]
    tm = _row_tile(T, 128)
    nt = T // tm
    row = lambda i: (i, 0)
    vec = lambda i: (l, 0, 0)
    dest3 = dest.reshape(nt, tm, TOP_K).transpose(0, 2, 1).reshape(nt, 1, TOP_K * tm)
    return pl.pallas_call(
        functools.partial(_combine_norm2_kernel, alpha=alpha),
        grid=(nt,),
        in_specs=[
            pl.BlockSpec((None, 1, TOP_K * tm), lambda i: (i, 0, 0), memory_space=pltpu.SMEM),
            pl.BlockSpec((None, 1, TOP_K * tm), lambda i: (jnp.minimum(i + 1, nt - 1), 0, 0),
                         memory_space=pltpu.SMEM),
            pl.BlockSpec((tm, D_MODEL), row),
            pl.BlockSpec(memory_space=pl.ANY),
            pl.BlockSpec((tm, TOP_K), row),
            pl.BlockSpec((tm, N_EXPERTS), row),
            pl.BlockSpec((None, N_EXPERTS, D_MODEL), vec),
            pl.BlockSpec((None, 1, D_MODEL), vec),
            pl.BlockSpec((None, 1, D_MODEL), vec),
        ],
        out_specs=[pl.BlockSpec((tm, D_MODEL), row), pl.BlockSpec((tm, D_MODEL), row)],
        out_shape=[jax.ShapeDtypeStruct((T, D_MODEL), F32), jax.ShapeDtypeStruct((T, D_MODEL), BF16)],
        scratch_shapes=[
            pltpu.VMEM((2, TOP_K * tm, D_MODEL // 2), jnp.uint32),
            pltpu.SemaphoreType.DMA((2,)),
        ],
        compiler_params=_params("arbitrary"),
        name="combine_norm2",
    )(dest3, dest3, x1, ys, p, gates, exp_b_d, ln_g3, ln_b3)


def _ple_kernel(xb_ref, x_ref, p_ref, wg_ref, wp_ref, o_ref, ob_ref):
    gate = _sigmoid(jnp.dot(xb_ref[...], wg_ref[...].astype(BF16), preferred_element_type=F32))
    pe = jnp.dot(p_ref[...].astype(BF16), wp_ref[...].astype(BF16), preferred_element_type=F32)
    out = x_ref[...] + gate * pe
    o_ref[...] = out
    ob_ref[...] = out.astype(BF16)


def ple(x2, x2b, p_l, ple_gate_w, ple_w, l):
    T = x2.shape[0]
    tm = _row_tile(T, 1088)
    tn = 512
    return pl.pallas_call(
        _ple_kernel,
        grid=(T // tm, D_MODEL // tn),
        in_specs=[
            pl.BlockSpec((tm, D_MODEL), lambda i, j: (i, 0)),
            pl.BlockSpec((tm, tn), lambda i, j: (i, j)),
            pl.BlockSpec((tm, PLE_DIM), lambda i, j: (i, 0)),
            pl.BlockSpec((None, D_MODEL, tn), lambda i, j: (l, 0, j)),
            pl.BlockSpec((None, PLE_DIM, tn), lambda i, j: (l, 0, j)),
        ],
        out_specs=[pl.BlockSpec((tm, tn), lambda i, j: (i, j)), pl.BlockSpec((tm, tn), lambda i, j: (i, j))],
        out_shape=[jax.ShapeDtypeStruct((T, D_MODEL), F32), jax.ShapeDtypeStruct((T, D_MODEL), BF16)],
        compiler_params=_params("parallel", "parallel"),
        name="ple",
    )(x2b, x2, p_l, ple_gate_w, ple_w)


def _tables(pos):
    posf = pos.astype(F32)[:, None]
    half = ROT_DIM_A // 2
    inv = jnp.float32(ROPE_THETA_A) ** (-jnp.arange(half, dtype=F32) / half)
    ang = posf * inv[None, :]
    cos, sin = jnp.cos(ang), jnp.sin(ang)
    n = pos.shape[0]
    ones = jnp.ones((n, HEAD_DIM_A - ROT_DIM_A), F32)
    zeros = jnp.zeros((n, HEAD_DIM_A - ROT_DIM_A), F32)
    z8 = jnp.zeros((n, half), F32)
    c64 = jnp.concatenate([cos, cos, ones], 1)
    lo64 = jnp.concatenate([-sin, z8, zeros], 1)
    hi64 = jnp.concatenate([z8, sin, zeros], 1)
    tab_a = jnp.concatenate([c64, c64, lo64, lo64, hi64, hi64], 1)
    hb = B_HEAD_DIM // 2
    invb = jnp.float32(RET_THETA) ** (-jnp.arange(hb, dtype=F32) / hb)
    angb = posf * invb[None, :]
    tab_b = jnp.concatenate([jnp.cos(angb), jnp.sin(angb)], 1)
    return tab_a, tab_b


def kernel(x_prompt, x_sample, cache_a_k, cache_a_v, state_b, p_prompt, p_sample, w_in, b_in, a_sinks, c_ln_g, c_ln_b, c_ws, c_wb, w_out, b_out, ln1_g, ln1_b, router_w, router_b, exp_w_gu, exp_b_gu, exp_w_d, exp_b_d, ln2_g, ln2_b, ple_w, ple_gate_w):
    depth = w_in.shape[0]
    alpha = (2 * depth) ** 0.25
    bp, S, _ = x_prompt.shape
    nb, Ls, _ = x_sample.shape
    assert bp == 1 and Ls == CHUNK and S % RET_BLOCK == 0
    win = cache_a_k.shape[2]
    n_s = nb * Ls

    log_gamma = jnp.log(1.0 - 2.0 ** (-5.0 - jnp.arange(B_HEADS, dtype=F32)))
    pos = jnp.concatenate([jnp.arange(S), jnp.tile(PAST_LEN + jnp.arange(Ls), nb)])
    tab_a, tab_b = _tables(pos)

    x = jnp.concatenate([x_prompt.reshape(S, D_MODEL), x_sample.reshape(n_s, D_MODEL)], 0)
    xb = x.astype(BF16)
    cache_k = cache_a_k.reshape(depth, nb, win, A_KV_WIDTH)
    cache_v = cache_a_v.reshape(depth, nb, win, A_KV_WIDTH)
    vec3 = lambda a: a.reshape(depth, 1, a.shape[-1])
    b_in3, b_out3 = vec3(b_in), vec3(b_out)
    c_ln_g3, c_ln_b3 = vec3(c_ln_g), vec3(c_ln_b)
    ln1_g3, ln1_b3, ln2_g3, ln2_b3 = vec3(ln1_g), vec3(ln1_b), vec3(ln2_g), vec3(ln2_b)
    router_b3 = vec3(router_b)
    c_wbt = jnp.swapaxes(c_wb, 1, 2)
    exp_b_gu4 = exp_b_gu.reshape(depth, N_EXPERTS, 1, 2 * EXPERT_DIM)

    ak_p, av_p, rb_p, ak_s, av_s, rb_s, cv_s = [], [], [], [], [], [], []
    for l in range(depth):
        h = in_proj(xb, w_in, b_in3, l)
        oa_p, kr_p = attn_prompt(h, tab_a, a_sinks, l, S)
        oa_s, kr_s = attn_sample(h, tab_a, a_sinks, cache_k, cache_v, l, S, nb, Ls)
        ob_p, r_p = ret_prompt(h, tab_b, log_gamma, S)
        ob_s, r_s = ret_sample(h, tab_b, log_gamma, state_b, l, S, nb, Ls)
        (oc_p,) = cmlp(h, c_ln_g3, c_ln_b3, c_ws, c_wbt, l, 0, S // CMLP_CHUNK, CMLP_CHUNK, False)
        oc_s, vn_s = cmlp(h, c_ln_g3, c_ln_b3, c_ws, c_wbt, l, S, nb, Ls, True)
        oa = jnp.concatenate([oa_p, oa_s], 0)
        ob = jnp.concatenate([ob_p, ob_s], 0)
        oc = jnp.concatenate([oc_p, oc_s], 0)
        h2 = out_proj(oa, ob, oc, w_out, b_out3, l)
        x1, gates, idx, p, rank, counts = norm1_router(x, h2, ln1_g3, ln1_b3, router_w, router_b3, l, alpha)

        dest, pad_row, plan = routing_plan(idx, rank, counts)
        xs = dispatch(x1, dest, pad_row, plan[1])
        a = expert_up(xs, exp_w_gu, exp_b_gu4, plan, l)
        ys = expert_down(a, exp_w_d, plan, l)
        x2, x2b = combine_norm2(x1, ys, dest, p, gates, exp_b_d, ln2_g3, ln2_b3, l, alpha)

        p_l = jnp.concatenate([p_prompt[l].reshape(S, PLE_DIM), p_sample[l].reshape(n_s, PLE_DIM)], 0)
        x, xb = ple(x2, x2b, p_l, ple_gate_w, ple_w, l)

        ak_p.append(kr_p[S - win:].reshape(1, win, A_KV_HEADS, HEAD_DIM_A))
        av_p.append(h[S - win:S, OFF_AV:OFF_AV + A_KV_WIDTH].reshape(1, win, A_KV_HEADS, HEAD_DIM_A))
        rb_p.append(r_p[None])
        k_all = jnp.concatenate([cache_k[l], kr_s.reshape(nb, Ls, A_KV_WIDTH)], 1)
        v_all = jnp.concatenate([cache_v[l], h[S:, OFF_AV:OFF_AV + A_KV_WIDTH].reshape(nb, Ls, A_KV_WIDTH)], 1)
        ak_s.append(k_all[:, -win:].reshape(nb, win, A_KV_HEADS, HEAD_DIM_A))
        av_s.append(v_all[:, -win:].reshape(nb, win, A_KV_HEADS, HEAD_DIM_A))
        rb_s.append(r_s)
        cv_s.append(vn_s.reshape(nb, Ls, C_WIDTH))

    return (x[:S].reshape(1, S, D_MODEL), x[S:].reshape(nb, Ls, D_MODEL),
            jnp.stack(ak_p), jnp.stack(av_p), jnp.stack(rb_p),
            jnp.stack(ak_s), jnp.stack(av_s), jnp.stack(rb_s), jnp.stack(cv_s))
```

```python
import functools

import jax
import jax.numpy as jnp
from jax import lax
from jax.experimental import pallas as pl
from jax.experimental.pallas import tpu as pltpu

D_MODEL = 4096
CHUNK = 64
PAST_LEN = 4096
HEAD_DIM_A = 64
A_WIDTH = 1024
A_KV_HEADS = 4
A_GROUP = 4
A_KV_WIDTH = 256
ROT_DIM_A = 16
ROPE_THETA_A = 500000.0
B_HEAD_DIM = 256
B_WIDTH = 2048
B_HEADS = 8
RET_THETA = 10000.0
RET_BLOCK = 256
C_WIDTH = 1024
C_GROUPS = 4
C_GROUP_DIM = 256
CMLP_CHUNK = 128
OFF_AQ, OFF_AK, OFF_AV = 0, 1024, 1280
OFF_BQ, OFF_BK, OFF_BV, OFF_BG = 1536, 3584, 5632, 7680
OFF_CU, OFF_CV = 9728, 10752
IN_COLS = 11776
N_EXPERTS = 32
TOP_K = 4
EXPERT_DIM = 1024
SWIGLU_LIMIT = 7.0
SWIGLU_ALPHA = 1.702
PLE_DIM = 256
LN_EPS = 1e-5
RMS_EPS = 1e-6

LANES = 128
MOE_TILE = 256
VMEM_LIMIT = 56 * 1024 * 1024

BF16 = jnp.bfloat16
F32 = jnp.float32


def _params(*sem):
    return pltpu.CompilerParams(dimension_semantics=sem, vmem_limit_bytes=VMEM_LIMIT)


def _row_tile(n, target):
    best = None
    for t in range(16, min(n, target) + 1, 16):
        if n % t == 0:
            best = t
    assert best is not None, n
    return best


def _layer_norm(x, g, b):
    mu = jnp.mean(x, -1, keepdims=True)
    xc = x - mu
    var = jnp.mean(xc * xc, -1, keepdims=True)
    return xc * lax.rsqrt(var + LN_EPS) * g + b


def _gelu(x):
    c = 0.7978845608028654
    return 0.5 * x * (1.0 + jnp.tanh(c * (x + 0.044715 * (x * x * x))))


def _sigmoid(x):
    return 1.0 / (1.0 + jnp.exp(-x))


def _in_proj_kernel(x_ref, w_ref, b_ref, o_ref):
    w = w_ref[...].astype(BF16)
    o_ref[...] = jnp.dot(x_ref[...], w, preferred_element_type=F32) + b_ref[...]


def in_proj(xb, w_in, b_in3, l):
    T = xb.shape[0]
    n = w_in.shape[2]
    tm = _row_tile(T, 1088)
    tn = 512
    return pl.pallas_call(
        _in_proj_kernel,
        grid=(T // tm, n // tn),
        in_specs=[
            pl.BlockSpec((tm, D_MODEL), lambda i, j: (i, 0)),
            pl.BlockSpec((None, D_MODEL, tn), lambda i, j: (l, 0, j)),
            pl.BlockSpec((None, 1, tn), lambda i, j: (l, 0, j)),
        ],
        out_specs=pl.BlockSpec((tm, tn), lambda i, j: (i, j)),
        out_shape=jax.ShapeDtypeStruct((T, n), F32),
        compiler_params=_params("parallel", "parallel"),
        name="in_proj",
    )(xb, w_in, b_in3)


def _out_proj_kernel(oa_ref, ob_ref, oc_ref, w_ref, b_ref, o_ref):
    w = w_ref[...].astype(BF16)
    acc = jnp.dot(oa_ref[...], w[:A_WIDTH], preferred_element_type=F32)
    acc += jnp.dot(ob_ref[...], w[A_WIDTH:A_WIDTH + B_WIDTH], preferred_element_type=F32)
    acc += jnp.dot(oc_ref[...], w[A_WIDTH + B_WIDTH:], preferred_element_type=F32)
    o_ref[...] = acc + b_ref[...]


def out_proj(oa, ob, oc, w_out, b_out3, l):
    T = oa.shape[0]
    tm = _row_tile(T, 1088)
    tn = 512
    return pl.pallas_call(
        _out_proj_kernel,
        grid=(T // tm, D_MODEL // tn),
        in_specs=[
            pl.BlockSpec((tm, A_WIDTH), lambda i, j: (i, 0)),
            pl.BlockSpec((tm, B_WIDTH), lambda i, j: (i, 0)),
            pl.BlockSpec((tm, C_WIDTH), lambda i, j: (i, 0)),
            pl.BlockSpec((None, D_MODEL, tn), lambda i, j: (l, 0, j)),
            pl.BlockSpec((None, 1, tn), lambda i, j: (l, 0, j)),
        ],
        out_specs=pl.BlockSpec((tm, tn), lambda i, j: (i, j)),
        out_shape=jax.ShapeDtypeStruct((T, D_MODEL), F32),
        compiler_params=_params("parallel", "parallel"),
        name="out_proj",
    )(oa, ob, oc, w_out, b_out3)


def _rot_a(x, tab):
    c, s_lo, s_hi = tab[:, :LANES], tab[:, LANES:2 * LANES], tab[:, 2 * LANES:]
    half = ROT_DIM_A // 2
    outs = []
    for j in range(x.shape[1] // LANES):
        xc = x[:, j * LANES:(j + 1) * LANES]
        up = pltpu.roll(xc, LANES - half, 1)
        dn = pltpu.roll(xc, half, 1)
        outs.append(xc * c + up * s_lo + dn * s_hi)
    return outs[0] if len(outs) == 1 else jnp.concatenate(outs, axis=1)


def _attend(q, k, v, valid, sink_ref, l, o_ref, acc_ref):
    R = q.shape[0]
    qb = q.astype(BF16)
    kb = k.astype(BF16)
    vb = v.astype(BF16)
    grp = lax.broadcasted_iota(jnp.int32, (A_GROUP * R, 1), 0) // R
    for kh in range(A_KV_HEADS):
        base = kh * A_GROUP
        qs = jnp.concatenate(
            [qb[:, (base + g) * HEAD_DIM_A:(base + g + 1) * HEAD_DIM_A] for g in range(A_GROUP)], axis=0)
        k_h = kb[:, kh * HEAD_DIM_A:(kh + 1) * HEAD_DIM_A]
        v_h = vb[:, kh * HEAD_DIM_A:(kh + 1) * HEAD_DIM_A]
        s = lax.dot_general(qs, k_h, (((1,), (1,)), ((), ())), preferred_element_type=F32)
        if valid is not None:
            s = jnp.where(valid, s, -jnp.inf)
        sink = jnp.zeros((A_GROUP * R, 1), F32)
        for g in range(A_GROUP):
            sink = jnp.where(grp == g, sink_ref[l, base + g], sink)
        m = jnp.maximum(jnp.max(s, -1, keepdims=True), sink)
        e = jnp.exp(s - m)
        p = e / (jnp.sum(e, -1, keepdims=True) + jnp.exp(sink - m))
        o = jnp.dot(p.astype(BF16), v_h, preferred_element_type=F32)
        for g in range(A_GROUP):
            h = base + g
            acc_ref[:, h * HEAD_DIM_A:(h + 1) * HEAD_DIM_A] = o[g * R:(g + 1) * R]
    o_ref[...] = acc_ref[...].astype(o_ref.dtype)


def _attn_prompt_kernel(sink_ref, q_ref, kp_ref, kc_ref, vp_ref, vc_ref, tq_ref, tp_ref, o_ref, kr_ref, acc_ref,
                        *, l):
    i = pl.program_id(0)
    R = q_ref.shape[0]
    tq = tq_ref[...]
    q = _rot_a(q_ref[...], tq) * (HEAD_DIM_A ** -0.5)
    kc = _rot_a(kc_ref[...], tq)
    kp = _rot_a(kp_ref[...], tp_ref[...])
    kr_ref[...] = kc
    k = jnp.concatenate([kp, kc], axis=0)
    v = jnp.concatenate([vp_ref[...], vc_ref[...]], axis=0)
    n_q = A_GROUP * R
    qc = (lax.broadcasted_iota(jnp.int32, (n_q, 2 * R), 0) % R) // CHUNK
    col = lax.broadcasted_iota(jnp.int32, (n_q, 2 * R), 1)
    kc_idx = col // CHUNK - R // CHUNK
    lo = jnp.where(i == 0, 0, -(R // CHUNK))
    valid = (kc_idx <= qc) & (kc_idx >= jnp.maximum(qc - 2, lo))
    _attend(q, k, v, valid, sink_ref, l, o_ref, acc_ref)


def attn_prompt(h, tab_a, sinks, l, S):
    R = 2 * CHUNK
    nb = S // R
    prev = lambda i: jnp.maximum(i - 1, 0)
    return pl.pallas_call(
        functools.partial(_attn_prompt_kernel, l=l),
        grid=(nb,),
        in_specs=[
            pl.BlockSpec(memory_space=pltpu.SMEM),
            pl.BlockSpec((R, A_WIDTH), lambda i: (i, 0)),
            pl.BlockSpec((R, A_KV_WIDTH), lambda i: (prev(i), OFF_AK // A_KV_WIDTH)),
            pl.BlockSpec((R, A_KV_WIDTH), lambda i: (i, OFF_AK // A_KV_WIDTH)),
            pl.BlockSpec((R, A_KV_WIDTH), lambda i: (prev(i), OFF_AV // A_KV_WIDTH)),
            pl.BlockSpec((R, A_KV_WIDTH), lambda i: (i, OFF_AV // A_KV_WIDTH)),
            pl.BlockSpec((R, 3 * LANES), lambda i: (i, 0)),
            pl.BlockSpec((R, 3 * LANES), lambda i: (prev(i), 0)),
        ],
        out_specs=[
            pl.BlockSpec((R, A_WIDTH), lambda i: (i, 0)),
            pl.BlockSpec((R, A_KV_WIDTH), lambda i: (i, 0)),
        ],
        out_shape=[
            jax.ShapeDtypeStruct((S, A_WIDTH), BF16),
            jax.ShapeDtypeStruct((S, A_KV_WIDTH), F32),
        ],
        scratch_shapes=[pltpu.VMEM((R, A_WIDTH), F32)],
        compiler_params=_params("parallel"),
        name="attn_prompt",
    )(sinks, h, h, h, h, h, tab_a, tab_a)


def _attn_sample_kernel(sink_ref, q_ref, kn_ref, vn_ref, kcache_ref, vcache_ref, tq_ref, o_ref, kr_ref, acc_ref,
                        *, l):
    tq = tq_ref[...]
    q = _rot_a(q_ref[...], tq) * (HEAD_DIM_A ** -0.5)
    kn = _rot_a(kn_ref[...], tq)
    kr_ref[...] = kn
    k = jnp.concatenate([kcache_ref[...], kn], axis=0)
    v = jnp.concatenate([vcache_ref[...], vn_ref[...]], axis=0)
    _attend(q, k, v, None, sink_ref, l, o_ref, acc_ref)


def attn_sample(h, tab_a, sinks, cache_k, cache_v, l, S, nb, Ls):
    win = cache_k.shape[2]
    r0 = S // Ls
    return pl.pallas_call(
        functools.partial(_attn_sample_kernel, l=l),
        grid=(nb,),
        in_specs=[
            pl.BlockSpec(memory_space=pltpu.SMEM),
            pl.BlockSpec((Ls, A_WIDTH), lambda b: (r0 + b, 0)),
            pl.BlockSpec((Ls, A_KV_WIDTH), lambda b: (r0 + b, OFF_AK // A_KV_WIDTH)),
            pl.BlockSpec((Ls, A_KV_WIDTH), lambda b: (r0 + b, OFF_AV // A_KV_WIDTH)),
            pl.BlockSpec((None, None, win, A_KV_WIDTH), lambda b: (l, b, 0, 0)),
            pl.BlockSpec((None, None, win, A_KV_WIDTH), lambda b: (l, b, 0, 0)),
            pl.BlockSpec((Ls, 3 * LANES), lambda b: (r0 + b, 0)),
        ],
        out_specs=[
            pl.BlockSpec((Ls, A_WIDTH), lambda b: (b, 0)),
            pl.BlockSpec((Ls, A_KV_WIDTH), lambda b: (b, 0)),
        ],
        out_shape=[
            jax.ShapeDtypeStruct((nb * Ls, A_WIDTH), BF16),
            jax.ShapeDtypeStruct((nb * Ls, A_KV_WIDTH), F32),
        ],
        scratch_shapes=[pltpu.VMEM((Ls, A_WIDTH), F32)],
        compiler_params=_params("parallel"),
        name="attn_sample",
    )(sinks, h, h, h, cache_k, cache_v, tab_a)


def _rot_b(x, cs):
    half = B_HEAD_DIM // 2
    c, s = cs[:, :half], cs[:, half:]
    x1, x2 = x[:, :half], x[:, half:]
    return jnp.concatenate([x1 * c - x2 * s, x2 * c + x1 * s], axis=1)


def _retention_step(lg, q, k, v, g, cs, r):
    L = q.shape[0]
    q = _rot_b(q, cs)
    k = _rot_b(k, cs) * (B_HEAD_DIM ** -0.5)
    ii = lax.broadcasted_iota(jnp.int32, (L, L), 0)
    jj = lax.broadcasted_iota(jnp.int32, (L, L), 1)
    diff = (ii - jj).astype(F32)
    decay = jnp.where(diff >= 0, jnp.exp(lg * jnp.maximum(diff, 0.0)), 0.0)
    row = lax.broadcasted_iota(jnp.int32, (L, B_HEAD_DIM), 0).astype(F32)
    qb = q.astype(BF16)
    vb = v.astype(BF16)
    scores = lax.dot_general(qb, k.astype(BF16), (((1,), (1,)), ((), ())), preferred_element_type=F32) * decay
    inner = jnp.dot(scores.astype(BF16), vb, preferred_element_type=F32)
    cross = jnp.dot(qb, r.astype(BF16), preferred_element_type=F32) * jnp.exp(lg * (row + 1.0))
    k_dec = k * jnp.exp(lg * (L - 1.0 - row))
    g_all = jnp.exp(lg * jnp.full((1, B_HEAD_DIM), float(L), F32))
    r_new = r * g_all + jnp.dot(k_dec.T.astype(BF16), vb, preferred_element_type=F32)
    o = inner + cross
    on = o * lax.rsqrt(jnp.mean(o * o, -1, keepdims=True) + RMS_EPS)
    return on * (g * _sigmoid(g)), r_new


RET_HEADS_PER_STEP = 2


def _ret_prompt_kernel(lg_ref, q_ref, k_ref, v_ref, g_ref, cs_ref, o_ref, r_ref, r_scr):
    hb = pl.program_id(0)
    c = pl.program_id(1)

    @pl.when(c == 0)
    def _():
        r_scr[...] = jnp.zeros_like(r_scr)

    cs = cs_ref[...]
    for u in range(RET_HEADS_PER_STEP):
        sl = slice(u * B_HEAD_DIM, (u + 1) * B_HEAD_DIM)
        o, r_new = _retention_step(lg_ref[hb * RET_HEADS_PER_STEP + u], q_ref[:, sl], k_ref[:, sl], v_ref[:, sl],
                                   g_ref[:, sl], cs, r_scr[u])
        o_ref[:, sl] = o.astype(o_ref.dtype)
        r_scr[u] = r_new
        r_ref[u] = r_new


def ret_prompt(h, tab_b, log_gamma, S):
    L = RET_BLOCK
    nc = S // L
    hps = RET_HEADS_PER_STEP
    w = hps * B_HEAD_DIM
    col = lambda off: (lambda hb, c: (c, off // w + hb))
    return pl.pallas_call(
        _ret_prompt_kernel,
        grid=(B_HEADS // hps, nc),
        in_specs=[
            pl.BlockSpec(memory_space=pltpu.SMEM),
            pl.BlockSpec((L, w), col(OFF_BQ)),
            pl.BlockSpec((L, w), col(OFF_BK)),
            pl.BlockSpec((L, w), col(OFF_BV)),
            pl.BlockSpec((L, w), col(OFF_BG)),
            pl.BlockSpec((L, B_HEAD_DIM), lambda hb, c: (c, 0)),
        ],
        out_specs=[
            pl.BlockSpec((L, w), lambda hb, c: (c, hb)),
            pl.BlockSpec((hps, B_HEAD_DIM, B_HEAD_DIM), lambda hb, c: (hb, 0, 0)),
        ],
        out_shape=[
            jax.ShapeDtypeStruct((S, B_WIDTH), BF16),
            jax.ShapeDtypeStruct((B_HEADS, B_HEAD_DIM, B_HEAD_DIM), F32),
        ],
        scratch_shapes=[pltpu.VMEM((hps, B_HEAD_DIM, B_HEAD_DIM), F32)],
        compiler_params=_params("parallel", "arbitrary"),
        name="ret_prompt",
    )(log_gamma, h, h, h, h, tab_b)


def _ret_sample_kernel(lg_ref, q_ref, k_ref, v_ref, g_ref, cs_ref, r0_ref, o_ref, r_ref):
    hb = pl.program_id(1)
    cs = cs_ref[...]
    for u in range(RET_HEADS_PER_STEP):
        sl = slice(u * B_HEAD_DIM, (u + 1) * B_HEAD_DIM)
        o, r_new = _retention_step(lg_ref[hb * RET_HEADS_PER_STEP + u], q_ref[:, sl], k_ref[:, sl], v_ref[:, sl],
                                   g_ref[:, sl], cs, r0_ref[u])
        o_ref[:, sl] = o.astype(o_ref.dtype)
        r_ref[u] = r_new


def ret_sample(h, tab_b, log_gamma, state_b, l, S, nb, Ls):
    r0 = S // Ls
    hps = RET_HEADS_PER_STEP
    w = hps * B_HEAD_DIM
    col = lambda off: (lambda b, hb: (r0 + b, off // w + hb))
    return pl.pallas_call(
        _ret_sample_kernel,
        grid=(nb, B_HEADS // hps),
        in_specs=[
            pl.BlockSpec(memory_space=pltpu.SMEM),
            pl.BlockSpec((Ls, w), col(OFF_BQ)),
            pl.BlockSpec((Ls, w), col(OFF_BK)),
            pl.BlockSpec((Ls, w), col(OFF_BV)),
            pl.BlockSpec((Ls, w), col(OFF_BG)),
            pl.BlockSpec((Ls, B_HEAD_DIM), lambda b, hb: (r0 + b, 0)),
            pl.BlockSpec((None, None, hps, B_HEAD_DIM, B_HEAD_DIM), lambda b, hb: (l, b, hb, 0, 0)),
        ],
        out_specs=[
            pl.BlockSpec((Ls, w), lambda b, hb: (b, hb)),
            pl.BlockSpec((None, hps, B_HEAD_DIM, B_HEAD_DIM), lambda b, hb: (b, hb, 0, 0)),
        ],
        out_shape=[
            jax.ShapeDtypeStruct((nb * Ls, B_WIDTH), BF16),
            jax.ShapeDtypeStruct((nb, B_HEADS, B_HEAD_DIM, B_HEAD_DIM), F32),
        ],
        compiler_params=_params("parallel", "parallel"),
        name="ret_sample",
    )(log_gamma, h, h, h, h, tab_b, state_b)


def _cmlp_kernel(u0_ref, u1_ref, v0_ref, v1_ref, g_ref, b_ref, ws_ref, wbt_ref, o_ref, *maybe_vn_ref):
    R = u0_ref.shape[0]
    u = _gelu(jnp.concatenate([u0_ref[...], u1_ref[...]], axis=1))
    v = jnp.concatenate([v0_ref[...], v1_ref[...]], axis=1)
    vn = _layer_norm(_gelu(v), g_ref[...], b_ref[...])
    if maybe_vn_ref:
        maybe_vn_ref[0][...] = vn
    vb = vn.astype(BF16)
    ii = lax.broadcasted_iota(jnp.int32, (R, R), 0) // CHUNK
    jj = lax.broadcasted_iota(jnp.int32, (R, R), 1) // CHUNK
    for gi in range(C_GROUPS):
        wm = jnp.where(jj <= ii, ws_ref[gi][:R, :R], 0.0).astype(BF16)
        sl = slice(gi * C_GROUP_DIM, (gi + 1) * C_GROUP_DIM)
        sv = jnp.dot(wm, vb[:, sl], preferred_element_type=F32) + wbt_ref[:R, gi:gi + 1]
        o_ref[:, sl] = (u[:, sl] * sv).astype(o_ref.dtype)


def cmlp(h, c_ln_g3, c_ln_b3, c_ws, c_wbt, l, row0, n_blocks, R, with_vn):
    b0 = row0 // R
    hw = C_WIDTH // 2
    out_shape = [jax.ShapeDtypeStruct((n_blocks * R, C_WIDTH), BF16)]
    out_specs = [pl.BlockSpec((R, C_WIDTH), lambda i: (i, 0))]
    if with_vn:
        out_shape.append(jax.ShapeDtypeStruct((n_blocks * R, C_WIDTH), F32))
        out_specs.append(pl.BlockSpec((R, C_WIDTH), lambda i: (i, 0)))
    return pl.pallas_call(
        _cmlp_kernel,
        grid=(n_blocks,),
        in_specs=[
            pl.BlockSpec((R, hw), lambda i: (b0 + i, OFF_CU // hw)),
            pl.BlockSpec((R, hw), lambda i: (b0 + i, OFF_CU // hw + 1)),
            pl.BlockSpec((R, hw), lambda i: (b0 + i, OFF_CV // hw)),
            pl.BlockSpec((R, hw), lambda i: (b0 + i, OFF_CV // hw + 1)),
            pl.BlockSpec((None, 1, C_WIDTH), lambda i: (l, 0, 0)),
            pl.BlockSpec((None, 1, C_WIDTH), lambda i: (l, 0, 0)),
            pl.BlockSpec((None, C_GROUPS, CMLP_CHUNK, CMLP_CHUNK), lambda i: (l, 0, 0, 0)),
            pl.BlockSpec((None, CMLP_CHUNK, C_GROUPS), lambda i: (l, 0, 0)),
        ],
        out_specs=out_specs,
        out_shape=out_shape,
        compiler_params=_params("parallel"),
        name="cmlp_vn" if with_vn else "cmlp",
    )(h, h, h, h, c_ln_g3, c_ln_b3, c_ws, c_wbt)


def _split_bf16(x):
    hi = x.astype(BF16)
    lo = (x - hi.astype(F32)).astype(BF16)
    return hi, lo


def _norm1_router_kernel(x_ref, h_ref, g_ref, b_ref, rw_ref, rb_ref,
                         x1_ref, gates_ref, idx_ref, p_ref, rank_ref, counts_ref, cnt_scr, *, alpha):
    @pl.when(pl.program_id(0) == 0)
    def _():
        cnt_scr[...] = jnp.zeros_like(cnt_scr)

    x1 = _layer_norm(alpha * x_ref[...] + h_ref[...], g_ref[...], b_ref[...])
    x1_ref[...] = x1
    xh, xl = _split_bf16(x1)
    wh, wl = _split_bf16(rw_ref[...])
    dot = functools.partial(jnp.dot, preferred_element_type=F32)
    logits = dot(xh, wh) + (dot(xh, wl) + dot(xl, wh)) + rb_ref[...]
    R = logits.shape[0]
    lane = lax.broadcasted_iota(jnp.int32, (R, N_EXPERTS), 1).astype(F32)
    lane_k = lax.broadcasted_iota(jnp.int32, (R, TOP_K), 1)
    work = logits
    vals, idxs = [], []
    for _ in range(TOP_K):
        m = jnp.max(work, -1, keepdims=True)
        ix = jnp.min(jnp.where(work == m, lane, float(N_EXPERTS)), -1, keepdims=True)
        vals.append(m)
        idxs.append(ix)
        work = jnp.where(lane == ix, -jnp.inf, work)
    es = [jnp.exp(v - vals[0]) for v in vals]
    tot = es[0] + es[1] + es[2] + es[3]
    gates = jnp.zeros((R, N_EXPERTS), F32)
    idx_out = jnp.zeros((R, TOP_K), jnp.int32)
    p_out = jnp.zeros((R, TOP_K), F32)
    for k in range(TOP_K):
        pk = es[k] / tot
        gates = jnp.where(lane == idxs[k], pk, gates)
        idx_out = jnp.where(lane_k == k, idxs[k].astype(jnp.int32), idx_out)
        p_out = jnp.where(lane_k == k, pk, p_out)
    gates_ref[...] = gates
    idx_ref[...] = idx_out
    p_ref[...] = p_out
    onehot = jnp.zeros((R, N_EXPERTS), F32)
    for k in range(TOP_K):
        onehot = jnp.where(lane == idxs[k], 1.0, onehot)
    before = lax.broadcasted_iota(jnp.int32, (R, R), 0) > lax.broadcasted_iota(jnp.int32, (R, R), 1)
    prefix = jnp.dot(jnp.where(before, 1.0, 0.0).astype(BF16), onehot.astype(BF16), preferred_element_type=F32)
    base = cnt_scr[...] + prefix
    rank_out = jnp.zeros((R, TOP_K), jnp.int32)
    for k in range(TOP_K):
        rk = jnp.sum(jnp.where(lane == idxs[k], base, 0.0), -1, keepdims=True)
        rank_out = jnp.where(lane_k == k, rk.astype(jnp.int32), rank_out)
    rank_ref[...] = rank_out
    cnt_scr[...] += jnp.sum(onehot, 0, keepdims=True)
    counts_ref[...] = cnt_scr[...]


def norm1_router(x, h2, ln_g3, ln_b3, router_w, router_b3, l, alpha):
    T = x.shape[0]
    tm = _row_tile(T, 256)
    row = lambda i: (i, 0)
    vec = lambda i: (l, 0, 0)
    return pl.pallas_call(
        functools.partial(_norm1_router_kernel, alpha=alpha),
        grid=(T // tm,),
        in_specs=[
            pl.BlockSpec((tm, D_MODEL), row),
            pl.BlockSpec((tm, D_MODEL), row),
            pl.BlockSpec((None, 1, D_MODEL), vec),
            pl.BlockSpec((None, 1, D_MODEL), vec),
            pl.BlockSpec((None, D_MODEL, N_EXPERTS), vec),
            pl.BlockSpec((None, 1, N_EXPERTS), vec),
        ],
        out_specs=[
            pl.BlockSpec((tm, D_MODEL), row),
            pl.BlockSpec((tm, N_EXPERTS), row),
            pl.BlockSpec((tm, TOP_K), row),
            pl.BlockSpec((tm, TOP_K), row),
            pl.BlockSpec((tm, TOP_K), row),
            pl.BlockSpec((1, N_EXPERTS), lambda i: (0, 0)),
        ],
        out_shape=[
            jax.ShapeDtypeStruct((T, D_MODEL), F32),
            jax.ShapeDtypeStruct((T, N_EXPERTS), F32),
            jax.ShapeDtypeStruct((T, TOP_K), jnp.int32),
            jax.ShapeDtypeStruct((T, TOP_K), F32),
            jax.ShapeDtypeStruct((T, TOP_K), jnp.int32),
            jax.ShapeDtypeStruct((1, N_EXPERTS), F32),
        ],
        scratch_shapes=[pltpu.VMEM((1, N_EXPERTS), F32)],
        compiler_params=_params("arbitrary"),
        name="norm1_router",
    )(x, h2, ln_g3, ln_b3, router_w, router_b3)


def routing_plan(idx, rank, counts):
    T = idx.shape[0]
    n_tiles = T * TOP_K // MOE_TILE + N_EXPERTS
    counts = counts.reshape(N_EXPERTS).astype(jnp.int32)
    padded = ((counts + MOE_TILE - 1) // MOE_TILE) * MOE_TILE
    pend = jnp.cumsum(padded)
    pstart = pend - padded
    experts = jnp.arange(N_EXPERTS, dtype=jnp.int32)
    dest = jnp.sum(jnp.where(idx[:, :, None] == experts, pstart, 0), -1) + rank
    tile_start = jnp.arange(n_tiles, dtype=jnp.int32) * MOE_TILE
    tile_e = jnp.minimum(jnp.sum((pend[None, :] <= tile_start[:, None]).astype(jnp.int32), 1), N_EXPERTS - 1)
    n_used = pend[-1] // MOE_TILE
    pad_row = jnp.where(padded > 0, pend - MOE_TILE, -1).astype(jnp.int32)
    tile = jnp.arange(n_tiles, dtype=jnp.int32)
    prev_e = jnp.concatenate([jnp.full((1,), -1, jnp.int32), tile_e[:-1]])
    first = ((tile < n_used) & (tile_e != prev_e)).astype(jnp.int32)
    run_end = jnp.sum(jnp.where(tile_e[:, None] == experts, pend // MOE_TILE, 0), -1)
    wraps = (run_end >= n_used).astype(jnp.int32)
    nxt_tile = jnp.where(wraps == 1, 0, jnp.minimum(run_end, n_tiles - 1))
    nxt_e = jnp.sum(jnp.where(nxt_tile[:, None] == tile[None, :], tile_e[None, :], 0), -1)
    plan = (tile_e, n_used.astype(jnp.int32).reshape(1), first, nxt_e.astype(jnp.int32), wraps)
    return dest, pad_row, plan


def _dma_rows(n_rows, body):
    unroll = 8
    assert n_rows % unroll == 0

    def trip(t, carry):
        for u in range(unroll):
            body(t * unroll + u)
        return carry

    lax.fori_loop(0, n_rows // unroll, trip, 0)


def _pack_bf16_pairs(x):
    c = x.shape[1] // 2
    lo = pltpu.bitcast(x[:, :c].astype(BF16).astype(F32), jnp.uint32)
    hi = pltpu.bitcast(x[:, c:].astype(BF16).astype(F32), jnp.uint32)
    return (lo >> 16) | (hi & jnp.uint32(0xFFFF0000))


def _unpack_bf16_pairs(p):
    lo = pltpu.bitcast(p << 16, F32)
    hi = pltpu.bitcast(p & jnp.uint32(0xFFFF0000), F32)
    return jnp.concatenate([lo, hi], axis=1)


def _dispatch_kernel(pad_ref, nu_ref, dest_ref, x_ref, xs_hbm, zeros_buf, packed_buf, zero_sem, row_sem):
    tm = x_ref.shape[0]
    n_tiles = xs_hbm.shape[0] // MOE_TILE

    def zero_copy(row):
        start = pl.multiple_of(row, MOE_TILE)
        return pltpu.make_async_copy(zeros_buf, xs_hbm.at[pl.ds(start, MOE_TILE), :], zero_sem)

    @pl.when(pl.program_id(0) == 0)
    def _():
        zeros_buf[...] = jnp.zeros_like(zeros_buf)
        for e in range(N_EXPERTS):
            @pl.when(pad_ref[e] >= 0)
            def _():
                zero_copy(pad_ref[e]).start()

        def start_tail(t, c):
            zero_copy(t * MOE_TILE).start()
            return c

        def wait_tail(t, c):
            zero_copy(t * MOE_TILE).wait()
            return c

        lax.fori_loop(nu_ref[0], n_tiles, start_tail, 0)
        for e in range(N_EXPERTS):
            @pl.when(pad_ref[e] >= 0)
            def _():
                zero_copy(pad_ref[e]).wait()
        lax.fori_loop(nu_ref[0], n_tiles, wait_tail, 0)

    i = pl.program_id(0)
    slot = i % 2
    packed_buf[slot] = _pack_bf16_pairs(x_ref[...])

    def start_row(r):
        for k in range(TOP_K):
            d = dest_ref[0, r * TOP_K + k]
            pltpu.make_async_copy(packed_buf.at[slot, pl.ds(r, 1), :], xs_hbm.at[pl.ds(d, 1), :],
                                  row_sem.at[slot]).start()

    _dma_rows(tm, start_row)

    def wait_rows(s):
        for k in range(TOP_K):
            pltpu.make_async_copy(packed_buf.at[s], xs_hbm.at[pl.ds(0, tm), :], row_sem.at[s]).wait()

    @pl.when(i > 0)
    def _():
        wait_rows(1 - slot)

    @pl.when(i == pl.num_programs(0) - 1)
    def _():
        wait_rows(slot)


def dispatch(x1, dest, pad_row, n_used):
    T = x1.shape[0]
    tm = _row_tile(T, 256)
    n_rows = (T * TOP_K // MOE_TILE + N_EXPERTS) * MOE_TILE
    dest3 = dest.reshape(T // tm, 1, tm * TOP_K)
    return pl.pallas_call(
        _dispatch_kernel,
        grid_spec=pltpu.PrefetchScalarGridSpec(
            num_scalar_prefetch=2,
            grid=(T // tm,),
            in_specs=[
                pl.BlockSpec((None, 1, tm * TOP_K), lambda i, *_: (i, 0, 0), memory_space=pltpu.SMEM),
                pl.BlockSpec((tm, D_MODEL), lambda i, *_: (i, 0)),
            ],
            out_specs=pl.BlockSpec(memory_space=pl.ANY),
            scratch_shapes=[
                pltpu.VMEM((MOE_TILE, D_MODEL // 2), jnp.uint32),
                pltpu.VMEM((2, tm, D_MODEL // 2), jnp.uint32),
                pltpu.SemaphoreType.DMA(()),
                pltpu.SemaphoreType.DMA((2,)),
            ],
        ),
        out_shape=jax.ShapeDtypeStruct((n_rows, D_MODEL // 2), jnp.uint32),
        compiler_params=_params("arbitrary"),
        name="dispatch",
    )(pad_row, n_used, dest3, x1)


def _switch_weights(first, very_first, own_e, own_pass, nxt_e, nxt_pass, has_next, slot_ref, copies):
    @pl.when(very_first)
    def _():
        slot_ref[0] = 1
        for c in copies(own_e, own_pass, 0):
            c.start()

    @pl.when(first)
    def _():
        slot = 1 - slot_ref[0]
        slot_ref[0] = slot
        for c in copies(own_e, own_pass, slot):
            c.wait()

        @pl.when(has_next)
        def _():
            for c in copies(nxt_e, nxt_pass, 1 - slot):
                c.start()


def _expert_up_kernel(te_ref, nu_ref, first_ref, nxt_ref, wraps_ref, x_ref, wgu_hbm, bg_ref, bu_ref, a_ref,
                      wbuf, wsem, slot_ref, *, l, tn):
    j = pl.program_id(0)
    m = pl.program_id(1)
    nj = pl.num_programs(0)

    def copies(e, jj, slot):
        cols = lambda part: pl.ds(pl.multiple_of(part * EXPERT_DIM + jj * tn, tn), tn)
        return [pltpu.make_async_copy(wgu_hbm.at[l, e, :, cols(part)], wbuf.at[slot, part], wsem.at[slot, part])
                for part in range(2)]

    @pl.when(m < nu_ref[0])
    def _():
        _switch_weights(first_ref[m] == 1, (j == 0) & (m == 0), te_ref[m], j,
                        nxt_ref[m], j + wraps_ref[m], j + wraps_ref[m] < nj, slot_ref, copies)
        slot = slot_ref[0]
        x = _unpack_bf16_pairs(x_ref[...]).astype(BF16)
        hg = jnp.dot(x, wbuf[slot, 0].astype(BF16), preferred_element_type=F32) + bg_ref[...]
        hu = jnp.dot(x, wbuf[slot, 1].astype(BF16), preferred_element_type=F32) + bu_ref[...]
        g = jnp.minimum(hg, SWIGLU_LIMIT)
        u = jnp.clip(hu, -SWIGLU_LIMIT, SWIGLU_LIMIT)
        a_ref[...] = ((u + 1.0) * g * _sigmoid(SWIGLU_ALPHA * g)).astype(a_ref.dtype)

    @pl.when(m >= nu_ref[0])
    def _():
        a_ref[...] = jnp.zeros_like(a_ref)


def expert_up(xs, exp_w_gu, exp_b_gu4, plan, l):
    tile_e, n_used, first, nxt_e, wraps = plan
    rows = xs.shape[0]
    n_tiles = rows // MOE_TILE
    tn = 512
    nj = EXPERT_DIM // tn
    mc = lambda m, nu: jnp.minimum(m, nu[0] - 1)
    return pl.pallas_call(
        functools.partial(_expert_up_kernel, l=l, tn=tn),
        grid_spec=pltpu.PrefetchScalarGridSpec(
            num_scalar_prefetch=5,
            grid=(nj, n_tiles),
            in_specs=[
                pl.BlockSpec((MOE_TILE, D_MODEL // 2), lambda j, m, te, nu, *_: (mc(m, nu), 0)),
                pl.BlockSpec(memory_space=pl.ANY),
                pl.BlockSpec((None, None, 1, tn), lambda j, m, te, nu, *_: (l, te[mc(m, nu)], 0, j)),
                pl.BlockSpec((None, None, 1, tn), lambda j, m, te, nu, *_: (l, te[mc(m, nu)], 0, nj + j)),
            ],
            out_specs=pl.BlockSpec((MOE_TILE, tn), lambda j, m, *_: (m, j)),
            scratch_shapes=[
                pltpu.VMEM((2, 2, D_MODEL, tn), F32),
                pltpu.SemaphoreType.DMA((2, 2)),
                pltpu.SMEM((1,), jnp.int32),
            ],
        ),
        out_shape=jax.ShapeDtypeStruct((rows, EXPERT_DIM), BF16),
        compiler_params=_params("arbitrary", "arbitrary"),
        name="expert_up",
    )(tile_e, n_used, first, nxt_e, wraps, xs, exp_w_gu, exp_b_gu4, exp_b_gu4)


def _expert_down_kernel(te_ref, nu_ref, first_ref, nxt_ref, wraps_ref, a_ref, wd_hbm, y_ref,
                        wbuf, wsem, slot_ref, *, l):
    m = pl.program_id(0)

    def copies(e, _, slot):
        return [pltpu.make_async_copy(wd_hbm.at[l, e], wbuf.at[slot], wsem.at[slot])]

    @pl.when(m < nu_ref[0])
    def _():
        _switch_weights(first_ref[m] == 1, m == 0, te_ref[m], 0, nxt_ref[m], 0, wraps_ref[m] == 0,
                        slot_ref, copies)
        y = jnp.dot(a_ref[...], wbuf[slot_ref[0]].astype(BF16), preferred_element_type=F32)
        y_ref[...] = _pack_bf16_pairs(y)

    @pl.when(m >= nu_ref[0])
    def _():
        y_ref[...] = jnp.zeros_like(y_ref)


def expert_down(a, exp_w_d, plan, l):
    tile_e, n_used, first, nxt_e, wraps = plan
    rows = a.shape[0]
    n_tiles = rows // MOE_TILE
    mc = lambda m, nu: jnp.minimum(m, nu[0] - 1)
    return pl.pallas_call(
        functools.partial(_expert_down_kernel, l=l),
        grid_spec=pltpu.PrefetchScalarGridSpec(
            num_scalar_prefetch=5,
            grid=(n_tiles,),
            in_specs=[
                pl.BlockSpec((MOE_TILE, EXPERT_DIM), lambda m, te, nu, *_: (mc(m, nu), 0)),
                pl.BlockSpec(memory_space=pl.ANY),
            ],
            out_specs=pl.BlockSpec((MOE_TILE, D_MODEL // 2), lambda m, *_: (m, 0)),
            scratch_shapes=[
                pltpu.VMEM((2, EXPERT_DIM, D_MODEL), F32),
                pltpu.SemaphoreType.DMA((2,)),
                pltpu.SMEM((1,), jnp.int32),
            ],
        ),
        out_shape=jax.ShapeDtypeStruct((rows, D_MODEL // 2), jnp.uint32),
        compiler_params=_params("arbitrary"),
        name="expert_down",
    )(tile_e, n_used, first, nxt_e, wraps, a, exp_w_d)


def _combine_norm2_kernel(dcur_ref, dnext_ref, x1_ref, ys_hbm, p_ref, gates_ref, bd_ref, g_ref, b_ref,
                          x2_ref, x2b_ref, rows_a, rows_b, sems, *, alpha):
    i = pl.program_id(0)
    n = pl.num_programs(0)
    tm = x1_ref.shape[0]
    n_rows = TOP_K * tm

    def row_copy(d_ref, r, buf, s):
        return pltpu.make_async_copy(ys_hbm.at[pl.ds(d_ref[0, r], 1), :], buf.at[pl.ds(r, 1), :], sems.at[s])

    def wait_rows(buf, s):
        pltpu.make_async_copy(ys_hbm.at[pl.ds(0, n_rows), :], buf, sems.at[s]).wait()

    @pl.when(i == 0)
    def _():
        _dma_rows(n_rows, lambda r: row_copy(dcur_ref, r, rows_a, 0).start())

    def step(cur, cur_s, nxt, nxt_s):
        wait_rows(cur, cur_s)
        for r in range(n_rows):
            row_copy(dnext_ref, r, nxt, nxt_s).start()
        p = p_ref[...]
        y = jnp.dot(gates_ref[...].astype(BF16), bd_ref[...].astype(BF16), preferred_element_type=F32)
        for k in range(TOP_K):
            y += p[:, k:k + 1] * _unpack_bf16_pairs(cur[k * tm:(k + 1) * tm, :])
        x2 = _layer_norm(alpha * x1_ref[...] + y, g_ref[...], b_ref[...])
        x2_ref[...] = x2
        x2b_ref[...] = x2.astype(BF16)

        @pl.when(i == n - 1)
        def _():
            wait_rows(nxt, nxt_s)

    @pl.when(i % 2 == 0)
    def _():
        step(rows_a, 0, rows_b, 1)

    @pl.when(i % 2 == 1)
    def _():
        step(rows_b, 1, rows_a, 0)


def combine_norm2(x1, ys, dest, p, gates, exp_b_d, ln_g3, ln_b3, l, alpha):
    T = x1.shape[0]
    tm = _row_tile(T, 128)
    nt = T // tm
    row = lambda i: (i, 0)
    vec = lambda i: (l, 0, 0)
    dest3 = dest.reshape(nt, tm, TOP_K).transpose(0, 2, 1).reshape(nt, 1, TOP_K * tm)
    return pl.pallas_call(
        functools.partial(_combine_norm2_kernel, alpha=alpha),
        grid=(nt,),
        in_specs=[
            pl.BlockSpec((None, 1, TOP_K * tm), lambda i: (i, 0, 0), memory_space=pltpu.SMEM),
            pl.BlockSpec((None, 1, TOP_K * tm), lambda i: (jnp.minimum(i + 1, nt - 1), 0, 0),
                         memory_space=pltpu.SMEM),
            pl.BlockSpec((tm, D_MODEL), row),
            pl.BlockSpec(memory_space=pl.ANY),
            pl.BlockSpec((tm, TOP_K), row),
            pl.BlockSpec((tm, N_EXPERTS), row),
            pl.BlockSpec((None, N_EXPERTS, D_MODEL), vec),
            pl.BlockSpec((None, 1, D_MODEL), vec),
            pl.BlockSpec((None, 1, D_MODEL), vec),
        ],
        out_specs=[pl.BlockSpec((tm, D_MODEL), row), pl.BlockSpec((tm, D_MODEL), row)],
        out_shape=[jax.ShapeDtypeStruct((T, D_MODEL), F32), jax.ShapeDtypeStruct((T, D_MODEL), BF16)],
        scratch_shapes=[
            pltpu.VMEM((TOP_K * tm, D_MODEL // 2), jnp.uint32),
            pltpu.VMEM((TOP_K * tm, D_MODEL // 2), jnp.uint32),
            pltpu.SemaphoreType.DMA((2,)),
        ],
        compiler_params=_params("arbitrary"),
        name="combine_norm2",
    )(dest3, dest3, x1, ys, p, gates, exp_b_d, ln_g3, ln_b3)


def _ple_kernel(xb_ref, x_ref, p_ref, wg_ref, wp_ref, o_ref, ob_ref):
    wg = wg_ref[...].astype(BF16)
    wp = wp_ref[...].astype(BF16)
    parts = 2
    rows = x_ref.shape[0] // parts
    for part in range(parts):
        sl = pl.ds(part * rows, rows)
        gate = _sigmoid(jnp.dot(xb_ref[sl, :], wg, preferred_element_type=F32))
        pe = jnp.dot(p_ref[sl, :].astype(BF16), wp, preferred_element_type=F32)
        out = x_ref[sl, :] + gate * pe
        o_ref[sl, :] = out
        ob_ref[sl, :] = out.astype(BF16)


def ple(x2, x2b, p_l, ple_gate_w, ple_w, l):
    T = x2.shape[0]
    tm = _row_tile(T, 1088)
    tn = 512
    return pl.pallas_call(
        _ple_kernel,
        grid=(T // tm, D_MODEL // tn),
        in_specs=[
            pl.BlockSpec((tm, D_MODEL), lambda i, j: (i, 0)),
            pl.BlockSpec((tm, tn), lambda i, j: (i, j)),
            pl.BlockSpec((tm, PLE_DIM), lambda i, j: (i, 0)),
            pl.BlockSpec((None, D_MODEL, tn), lambda i, j: (l, 0, j)),
            pl.BlockSpec((None, PLE_DIM, tn), lambda i, j: (l, 0, j)),
        ],
        out_specs=[pl.BlockSpec((tm, tn), lambda i, j: (i, j)), pl.BlockSpec((tm, tn), lambda i, j: (i, j))],
        out_shape=[jax.ShapeDtypeStruct((T, D_MODEL), F32), jax.ShapeDtypeStruct((T, D_MODEL), BF16)],
        compiler_params=_params("parallel", "parallel"),
        name="ple",
    )(x2b, x2, p_l, ple_gate_w, ple_w)


def _tables(pos):
    posf = pos.astype(F32)[:, None]
    half = ROT_DIM_A // 2
    inv = jnp.float32(ROPE_THETA_A) ** (-jnp.arange(half, dtype=F32) / half)
    ang = posf * inv[None, :]
    cos, sin = jnp.cos(ang), jnp.sin(ang)
    n = pos.shape[0]
    ones = jnp.ones((n, HEAD_DIM_A - ROT_DIM_A), F32)
    zeros = jnp.zeros((n, HEAD_DIM_A - ROT_DIM_A), F32)
    z8 = jnp.zeros((n, half), F32)
    c64 = jnp.concatenate([cos, cos, ones], 1)
    lo64 = jnp.concatenate([-sin, z8, zeros], 1)
    hi64 = jnp.concatenate([z8, sin, zeros], 1)
    tab_a = jnp.concatenate([c64, c64, lo64, lo64, hi64, hi64], 1)
    hb = B_HEAD_DIM // 2
    invb = jnp.float32(RET_THETA) ** (-jnp.arange(hb, dtype=F32) / hb)
    angb = posf * invb[None, :]
    tab_b = jnp.concatenate([jnp.cos(angb), jnp.sin(angb)], 1)
    return tab_a, tab_b


def kernel(x_prompt, x_sample, cache_a_k, cache_a_v, state_b, p_prompt, p_sample, w_in, b_in, a_sinks, c_ln_g, c_ln_b, c_ws, c_wb, w_out, b_out, ln1_g, ln1_b, router_w, router_b, exp_w_gu, exp_b_gu, exp_w_d, exp_b_d, ln2_g, ln2_b, ple_w, ple_gate_w):
    depth = w_in.shape[0]
    alpha = (2 * depth) ** 0.25
    bp, S, _ = x_prompt.shape
    nb, Ls, _ = x_sample.shape
    assert bp == 1 and Ls == CHUNK and S % RET_BLOCK == 0
    win = cache_a_k.shape[2]
    n_s = nb * Ls

    log_gamma = jnp.log(1.0 - 2.0 ** (-5.0 - jnp.arange(B_HEADS, dtype=F32)))
    pos = jnp.concatenate([jnp.arange(S), jnp.tile(PAST_LEN + jnp.arange(Ls), nb)])
    tab_a, tab_b = _tables(pos)

    x = jnp.concatenate([x_prompt.reshape(S, D_MODEL), x_sample.reshape(n_s, D_MODEL)], 0)
    xb = x.astype(BF16)
    cache_k = cache_a_k.reshape(depth, nb, win, A_KV_WIDTH)
    cache_v = cache_a_v.reshape(depth, nb, win, A_KV_WIDTH)
    vec3 = lambda a: a.reshape(depth, 1, a.shape[-1])
    b_in3, b_out3 = vec3(b_in), vec3(b_out)
    c_ln_g3, c_ln_b3 = vec3(c_ln_g), vec3(c_ln_b)
    ln1_g3, ln1_b3, ln2_g3, ln2_b3 = vec3(ln1_g), vec3(ln1_b), vec3(ln2_g), vec3(ln2_b)
    router_b3 = vec3(router_b)
    c_wbt = jnp.swapaxes(c_wb, 1, 2)
    exp_b_gu4 = exp_b_gu.reshape(depth, N_EXPERTS, 1, 2 * EXPERT_DIM)

    ak_p, av_p, rb_p, ak_s, av_s, rb_s, cv_s = [], [], [], [], [], [], []
    for l in range(depth):
        h = in_proj(xb, w_in, b_in3, l)
        oa_p, kr_p = attn_prompt(h, tab_a, a_sinks, l, S)
        oa_s, kr_s = attn_sample(h, tab_a, a_sinks, cache_k, cache_v, l, S, nb, Ls)
        ob_p, r_p = ret_prompt(h, tab_b, log_gamma, S)
        ob_s, r_s = ret_sample(h, tab_b, log_gamma, state_b, l, S, nb, Ls)
        (oc_p,) = cmlp(h, c_ln_g3, c_ln_b3, c_ws, c_wbt, l, 0, S // CMLP_CHUNK, CMLP_CHUNK, False)
        oc_s, vn_s = cmlp(h, c_ln_g3, c_ln_b3, c_ws, c_wbt, l, S, nb, Ls, True)
        oa = jnp.concatenate([oa_p, oa_s], 0)
        ob = jnp.concatenate([ob_p, ob_s], 0)
        oc = jnp.concatenate([oc_p, oc_s], 0)
        h2 = out_proj(oa, ob, oc, w_out, b_out3, l)
        x1, gates, idx, p, rank, counts = norm1_router(x, h2, ln1_g3, ln1_b3, router_w, router_b3, l, alpha)

        dest, pad_row, plan = routing_plan(idx, rank, counts)
        xs = dispatch(x1, dest, pad_row, plan[1])
        a = expert_up(xs, exp_w_gu, exp_b_gu4, plan, l)
        ys = expert_down(a, exp_w_d, plan, l)
        x2, x2b = combine_norm2(x1, ys, dest, p, gates, exp_b_d, ln2_g3, ln2_b3, l, alpha)

        p_l = jnp.concatenate([p_prompt[l].reshape(S, PLE_DIM), p_sample[l].reshape(n_s, PLE_DIM)], 0)
        x, xb = ple(x2, x2b, p_l, ple_gate_w, ple_w, l)

        ak_p.append(kr_p[S - win:].reshape(1, win, A_KV_HEADS, HEAD_DIM_A))
        av_p.append(h[S - win:S, OFF_AV:OFF_AV + A_KV_WIDTH].reshape(1, win, A_KV_HEADS, HEAD_DIM_A))
        rb_p.append(r_p[None])
        k_all = jnp.concatenate([cache_k[l], kr_s.reshape(nb, Ls, A_KV_WIDTH)], 1)
        v_all = jnp.concatenate([cache_v[l], h[S:, OFF_AV:OFF_AV + A_KV_WIDTH].reshape(nb, Ls, A_KV_WIDTH)], 1)
        ak_s.append(k_all[:, -win:].reshape(nb, win, A_KV_HEADS, HEAD_DIM_A))
        av_s.append(v_all[:, -win:].reshape(nb, win, A_KV_HEADS, HEAD_DIM_A))
        rb_s.append(r_s)
        cv_s.append(vn_s.reshape(nb, Ls, C_WIDTH))

    return (x[:S].reshape(1, S, D_MODEL), x[S:].reshape(nb, Ls, D_MODEL),
            jnp.stack(ak_p), jnp.stack(av_p), jnp.stack(rb_p),
            jnp.stack(ak_s), jnp.stack(av_s), jnp.stack(rb_s), jnp.stack(cv_s))
```

```python
import functools

import jax
import jax.numpy as jnp
from jax import lax
from jax.experimental import pallas as pl
from jax.experimental.pallas import tpu as pltpu

D_MODEL = 4096
CHUNK = 64
PAST_LEN = 4096
HEAD_DIM_A = 64
A_WIDTH = 1024
A_KV_HEADS = 4
A_GROUP = 4
A_KV_WIDTH = 256
ROT_DIM_A = 16
ROPE_THETA_A = 500000.0
B_HEAD_DIM = 256
B_WIDTH = 2048
B_HEADS = 8
RET_THETA = 10000.0
RET_BLOCK = 256
C_WIDTH = 1024
C_GROUPS = 4
C_GROUP_DIM = 256
CMLP_CHUNK = 128
OFF_AQ, OFF_AK, OFF_AV = 0, 1024, 1280
OFF_BQ, OFF_BK, OFF_BV, OFF_BG = 1536, 3584, 5632, 7680
OFF_CU, OFF_CV = 9728, 10752
IN_COLS = 11776
N_EXPERTS = 32
TOP_K = 4
EXPERT_DIM = 1024
SWIGLU_LIMIT = 7.0
SWIGLU_ALPHA = 1.702
PLE_DIM = 256
LN_EPS = 1e-5
RMS_EPS = 1e-6

LANES = 128
MOE_TILE = 256
VMEM_LIMIT = 56 * 1024 * 1024

BF16 = jnp.bfloat16
F32 = jnp.float32


def _params(*sem):
    return pltpu.CompilerParams(dimension_semantics=sem, vmem_limit_bytes=VMEM_LIMIT)


def _row_tile(n, target):
    best = None
    for t in range(16, min(n, target) + 1, 16):
        if n % t == 0:
            best = t
    assert best is not None, n
    return best


def _layer_norm(x, g, b):
    mu = jnp.mean(x, -1, keepdims=True)
    xc = x - mu
    var = jnp.mean(xc * xc, -1, keepdims=True)
    return xc * lax.rsqrt(var + LN_EPS) * g + b


def _gelu(x):
    c = 0.7978845608028654
    return 0.5 * x * (1.0 + jnp.tanh(c * (x + 0.044715 * (x * x * x))))


def _sigmoid(x):
    return 1.0 / (1.0 + jnp.exp(-x))


def _in_proj_kernel(x_ref, w_ref, b_ref, o_ref):
    w = w_ref[...].astype(BF16)
    o_ref[...] = jnp.dot(x_ref[...], w, preferred_element_type=F32) + b_ref[...]


def in_proj(xb, w_in, b_in3, l):
    T = xb.shape[0]
    n = w_in.shape[2]
    tm = _row_tile(T, 1088)
    tn = 512
    return pl.pallas_call(
        _in_proj_kernel,
        grid=(T // tm, n // tn),
        in_specs=[
            pl.BlockSpec((tm, D_MODEL), lambda i, j: (i, 0)),
            pl.BlockSpec((None, D_MODEL, tn), lambda i, j: (l, 0, j)),
            pl.BlockSpec((None, 1, tn), lambda i, j: (l, 0, j)),
        ],
        out_specs=pl.BlockSpec((tm, tn), lambda i, j: (i, j)),
        out_shape=jax.ShapeDtypeStruct((T, n), F32),
        compiler_params=_params("parallel", "parallel"),
        name="in_proj",
    )(xb, w_in, b_in3)


def _out_proj_kernel(oa_ref, ob_ref, oc_ref, w_ref, b_ref, o_ref):
    w = w_ref[...].astype(BF16)
    acc = jnp.dot(oa_ref[...], w[:A_WIDTH], preferred_element_type=F32)
    acc += jnp.dot(ob_ref[...], w[A_WIDTH:A_WIDTH + B_WIDTH], preferred_element_type=F32)
    acc += jnp.dot(oc_ref[...], w[A_WIDTH + B_WIDTH:], preferred_element_type=F32)
    o_ref[...] = acc + b_ref[...]


def out_proj(oa, ob, oc, w_out, b_out3, l):
    T = oa.shape[0]
    tm = _row_tile(T, 1088)
    tn = 512
    return pl.pallas_call(
        _out_proj_kernel,
        grid=(T // tm, D_MODEL // tn),
        in_specs=[
            pl.BlockSpec((tm, A_WIDTH), lambda i, j: (i, 0)),
            pl.BlockSpec((tm, B_WIDTH), lambda i, j: (i, 0)),
            pl.BlockSpec((tm, C_WIDTH), lambda i, j: (i, 0)),
            pl.BlockSpec((None, D_MODEL, tn), lambda i, j: (l, 0, j)),
            pl.BlockSpec((None, 1, tn), lambda i, j: (l, 0, j)),
        ],
        out_specs=pl.BlockSpec((tm, tn), lambda i, j: (i, j)),
        out_shape=jax.ShapeDtypeStruct((T, D_MODEL), F32),
        compiler_params=_params("parallel", "parallel"),
        name="out_proj",
    )(oa, ob, oc, w_out, b_out3)


def _rot_a(x, tab):
    c, s_lo, s_hi = tab[:, :LANES], tab[:, LANES:2 * LANES], tab[:, 2 * LANES:]
    half = ROT_DIM_A // 2
    outs = []
    for j in range(x.shape[1] // LANES):
        xc = x[:, j * LANES:(j + 1) * LANES]
        up = pltpu.roll(xc, LANES - half, 1)
        dn = pltpu.roll(xc, half, 1)
        outs.append(xc * c + up * s_lo + dn * s_hi)
    return outs[0] if len(outs) == 1 else jnp.concatenate(outs, axis=1)


def _attend(q, k, v, valid, sink_ref, l, o_ref, acc_ref):
    R = q.shape[0]
    qb = q.astype(BF16)
    kb = k.astype(BF16)
    vb = v.astype(BF16)
    grp = lax.broadcasted_iota(jnp.int32, (A_GROUP * R, 1), 0) // R
    for kh in range(A_KV_HEADS):
        base = kh * A_GROUP
        qs = jnp.concatenate(
            [qb[:, (base + g) * HEAD_DIM_A:(base + g + 1) * HEAD_DIM_A] for g in range(A_GROUP)], axis=0)
        k_h = kb[:, kh * HEAD_DIM_A:(kh + 1) * HEAD_DIM_A]
        v_h = vb[:, kh * HEAD_DIM_A:(kh + 1) * HEAD_DIM_A]
        s = lax.dot_general(qs, k_h, (((1,), (1,)), ((), ())), preferred_element_type=F32)
        if valid is not None:
            s = jnp.where(valid, s, -jnp.inf)
        sink = jnp.zeros((A_GROUP * R, 1), F32)
        for g in range(A_GROUP):
            sink = jnp.where(grp == g, sink_ref[l, base + g], sink)
        m = jnp.maximum(jnp.max(s, -1, keepdims=True), sink)
        e = jnp.exp(s - m)
        p = e / (jnp.sum(e, -1, keepdims=True) + jnp.exp(sink - m))
        o = jnp.dot(p.astype(BF16), v_h, preferred_element_type=F32)
        for g in range(A_GROUP):
            h = base + g
            acc_ref[:, h * HEAD_DIM_A:(h + 1) * HEAD_DIM_A] = o[g * R:(g + 1) * R]
    o_ref[...] = acc_ref[...].astype(o_ref.dtype)


def _attn_prompt_kernel(sink_ref, q_ref, kp_ref, kc_ref, vp_ref, vc_ref, tq_ref, tp_ref, o_ref, kr_ref, acc_ref,
                        *, l):
    i = pl.program_id(0)
    R = q_ref.shape[0]
    tq = tq_ref[...]
    q = _rot_a(q_ref[...], tq) * (HEAD_DIM_A ** -0.5)
    kc = _rot_a(kc_ref[...], tq)
    kp = _rot_a(kp_ref[...], tp_ref[...])
    kr_ref[...] = kc
    k = jnp.concatenate([kp, kc], axis=0)
    v = jnp.concatenate([vp_ref[...], vc_ref[...]], axis=0)
    n_q = A_GROUP * R
    qc = (lax.broadcasted_iota(jnp.int32, (n_q, 2 * R), 0) % R) // CHUNK
    col = lax.broadcasted_iota(jnp.int32, (n_q, 2 * R), 1)
    kc_idx = col // CHUNK - R // CHUNK
    lo = jnp.where(i == 0, 0, -(R // CHUNK))
    valid = (kc_idx <= qc) & (kc_idx >= jnp.maximum(qc - 2, lo))
    _attend(q, k, v, valid, sink_ref, l, o_ref, acc_ref)


def attn_prompt(h, tab_a, sinks, l, S):
    R = 2 * CHUNK
    nb = S // R
    prev = lambda i: jnp.maximum(i - 1, 0)
    return pl.pallas_call(
        functools.partial(_attn_prompt_kernel, l=l),
        grid=(nb,),
        in_specs=[
            pl.BlockSpec(memory_space=pltpu.SMEM),
            pl.BlockSpec((R, A_WIDTH), lambda i: (i, 0)),
            pl.BlockSpec((R, A_KV_WIDTH), lambda i: (prev(i), OFF_AK // A_KV_WIDTH)),
            pl.BlockSpec((R, A_KV_WIDTH), lambda i: (i, OFF_AK // A_KV_WIDTH)),
            pl.BlockSpec((R, A_KV_WIDTH), lambda i: (prev(i), OFF_AV // A_KV_WIDTH)),
            pl.BlockSpec((R, A_KV_WIDTH), lambda i: (i, OFF_AV // A_KV_WIDTH)),
            pl.BlockSpec((R, 3 * LANES), lambda i: (i, 0)),
            pl.BlockSpec((R, 3 * LANES), lambda i: (prev(i), 0)),
        ],
        out_specs=[
            pl.BlockSpec((R, A_WIDTH), lambda i: (i, 0)),
            pl.BlockSpec((R, A_KV_WIDTH), lambda i: (i, 0)),
        ],
        out_shape=[
            jax.ShapeDtypeStruct((S, A_WIDTH), BF16),
            jax.ShapeDtypeStruct((S, A_KV_WIDTH), F32),
        ],
        scratch_shapes=[pltpu.VMEM((R, A_WIDTH), F32)],
        compiler_params=_params("parallel"),
        name="attn_prompt",
    )(sinks, h, h, h, h, h, tab_a, tab_a)


def _attn_sample_kernel(sink_ref, q_ref, kn_ref, vn_ref, kcache_ref, vcache_ref, tq_ref, o_ref, kr_ref, acc_ref,
                        *, l):
    tq = tq_ref[...]
    q = _rot_a(q_ref[...], tq) * (HEAD_DIM_A ** -0.5)
    kn = _rot_a(kn_ref[...], tq)
    kr_ref[...] = kn
    k = jnp.concatenate([kcache_ref[...], kn], axis=0)
    v = jnp.concatenate([vcache_ref[...], vn_ref[...]], axis=0)
    _attend(q, k, v, None, sink_ref, l, o_ref, acc_ref)


def attn_sample(h, tab_a, sinks, cache_k, cache_v, l, S, nb, Ls):
    win = cache_k.shape[2]
    r0 = S // Ls
    return pl.pallas_call(
        functools.partial(_attn_sample_kernel, l=l),
        grid=(nb,),
        in_specs=[
            pl.BlockSpec(memory_space=pltpu.SMEM),
            pl.BlockSpec((Ls, A_WIDTH), lambda b: (r0 + b, 0)),
            pl.BlockSpec((Ls, A_KV_WIDTH), lambda b: (r0 + b, OFF_AK // A_KV_WIDTH)),
            pl.BlockSpec((Ls, A_KV_WIDTH), lambda b: (r0 + b, OFF_AV // A_KV_WIDTH)),
            pl.BlockSpec((None, None, win, A_KV_WIDTH), lambda b: (l, b, 0, 0)),
            pl.BlockSpec((None, None, win, A_KV_WIDTH), lambda b: (l, b, 0, 0)),
            pl.BlockSpec((Ls, 3 * LANES), lambda b: (r0 + b, 0)),
        ],
        out_specs=[
            pl.BlockSpec((Ls, A_WIDTH), lambda b: (b, 0)),
            pl.BlockSpec((Ls, A_KV_WIDTH), lambda b: (b, 0)),
        ],
        out_shape=[
            jax.ShapeDtypeStruct((nb * Ls, A_WIDTH), BF16),
            jax.ShapeDtypeStruct((nb * Ls, A_KV_WIDTH), F32),
        ],
        scratch_shapes=[pltpu.VMEM((Ls, A_WIDTH), F32)],
        compiler_params=_params("parallel"),
        name="attn_sample",
    )(sinks, h, h, h, cache_k, cache_v, tab_a)


def _rot_b(x, cs):
    half = B_HEAD_DIM // 2
    c, s = cs[:, :half], cs[:, half:]
    x1, x2 = x[:, :half], x[:, half:]
    return jnp.concatenate([x1 * c - x2 * s, x2 * c + x1 * s], axis=1)


def _retention_step(lg, q, k, v, g, cs, r):
    L = q.shape[0]
    q = _rot_b(q, cs)
    k = _rot_b(k, cs) * (B_HEAD_DIM ** -0.5)
    ii = lax.broadcasted_iota(jnp.int32, (L, L), 0)
    jj = lax.broadcasted_iota(jnp.int32, (L, L), 1)
    diff = (ii - jj).astype(F32)
    decay = jnp.where(diff >= 0, jnp.exp(lg * jnp.maximum(diff, 0.0)), 0.0)
    row = lax.broadcasted_iota(jnp.int32, (L, B_HEAD_DIM), 0).astype(F32)
    qb = q.astype(BF16)
    vb = v.astype(BF16)
    scores = lax.dot_general(qb, k.astype(BF16), (((1,), (1,)), ((), ())), preferred_element_type=F32) * decay
    inner = jnp.dot(scores.astype(BF16), vb, preferred_element_type=F32)
    cross = jnp.dot(qb, r.astype(BF16), preferred_element_type=F32) * jnp.exp(lg * (row + 1.0))
    k_dec = k * jnp.exp(lg * (L - 1.0 - row))
    g_all = jnp.exp(lg * jnp.full((1, B_HEAD_DIM), float(L), F32))
    r_new = r * g_all + jnp.dot(k_dec.T.astype(BF16), vb, preferred_element_type=F32)
    o = inner + cross
    on = o * lax.rsqrt(jnp.mean(o * o, -1, keepdims=True) + RMS_EPS)
    return on * (g * _sigmoid(g)), r_new


RET_HEADS_PER_STEP = 2


def _ret_prompt_kernel(lg_ref, q_ref, k_ref, v_ref, g_ref, cs_ref, o_ref, r_ref, r_scr):
    hb = pl.program_id(0)
    c = pl.program_id(1)

    @pl.when(c == 0)
    def _():
        r_scr[...] = jnp.zeros_like(r_scr)

    cs = cs_ref[...]
    for u in range(RET_HEADS_PER_STEP):
        sl = slice(u * B_HEAD_DIM, (u + 1) * B_HEAD_DIM)
        o, r_new = _retention_step(lg_ref[hb * RET_HEADS_PER_STEP + u], q_ref[:, sl], k_ref[:, sl], v_ref[:, sl],
                                   g_ref[:, sl], cs, r_scr[u])
        o_ref[:, sl] = o.astype(o_ref.dtype)
        r_scr[u] = r_new
        r_ref[u] = r_new


def ret_prompt(h, tab_b, log_gamma, S):
    L = RET_BLOCK
    nc = S // L
    hps = RET_HEADS_PER_STEP
    w = hps * B_HEAD_DIM
    col = lambda off: (lambda hb, c: (c, off // w + hb))
    return pl.pallas_call(
        _ret_prompt_kernel,
        grid=(B_HEADS // hps, nc),
        in_specs=[
            pl.BlockSpec(memory_space=pltpu.SMEM),
            pl.BlockSpec((L, w), col(OFF_BQ)),
            pl.BlockSpec((L, w), col(OFF_BK)),
            pl.BlockSpec((L, w), col(OFF_BV)),
            pl.BlockSpec((L, w), col(OFF_BG)),
            pl.BlockSpec((L, B_HEAD_DIM), lambda hb, c: (c, 0)),
        ],
        out_specs=[
            pl.BlockSpec((L, w), lambda hb, c: (c, hb)),
            pl.BlockSpec((hps, B_HEAD_DIM, B_HEAD_DIM), lambda hb, c: (hb, 0, 0)),
        ],
        out_shape=[
            jax.ShapeDtypeStruct((S, B_WIDTH), BF16),
            jax.ShapeDtypeStruct((B_HEADS, B_HEAD_DIM, B_HEAD_DIM), F32),
        ],
        scratch_shapes=[pltpu.VMEM((hps, B_HEAD_DIM, B_HEAD_DIM), F32)],
        compiler_params=_params("parallel", "arbitrary"),
        name="ret_prompt",
    )(log_gamma, h, h, h, h, tab_b)


def _ret_sample_kernel(lg_ref, q_ref, k_ref, v_ref, g_ref, cs_ref, r0_ref, o_ref, r_ref):
    hb = pl.program_id(1)
    cs = cs_ref[...]
    for u in range(RET_HEADS_PER_STEP):
        sl = slice(u * B_HEAD_DIM, (u + 1) * B_HEAD_DIM)
        o, r_new = _retention_step(lg_ref[hb * RET_HEADS_PER_STEP + u], q_ref[:, sl], k_ref[:, sl], v_ref[:, sl],
                                   g_ref[:, sl], cs, r0_ref[u])
        o_ref[:, sl] = o.astype(o_ref.dtype)
        r_ref[u] = r_new


def ret_sample(h, tab_b, log_gamma, state_b, l, S, nb, Ls):
    r0 = S // Ls
    hps = RET_HEADS_PER_STEP
    w = hps * B_HEAD_DIM
    col = lambda off: (lambda b, hb: (r0 + b, off // w + hb))
    return pl.pallas_call(
        _ret_sample_kernel,
        grid=(nb, B_HEADS // hps),
        in_specs=[
            pl.BlockSpec(memory_space=pltpu.SMEM),
            pl.BlockSpec((Ls, w), col(OFF_BQ)),
            pl.BlockSpec((Ls, w), col(OFF_BK)),
            pl.BlockSpec((Ls, w), col(OFF_BV)),
            pl.BlockSpec((Ls, w), col(OFF_BG)),
            pl.BlockSpec((Ls, B_HEAD_DIM), lambda b, hb: (r0 + b, 0)),
            pl.BlockSpec((None, None, hps, B_HEAD_DIM, B_HEAD_DIM), lambda b, hb: (l, b, hb, 0, 0)),
        ],
        out_specs=[
            pl.BlockSpec((Ls, w), lambda b, hb: (b, hb)),
            pl.BlockSpec((None, hps, B_HEAD_DIM, B_HEAD_DIM), lambda b, hb: (b, hb, 0, 0)),
        ],
        out_shape=[
            jax.ShapeDtypeStruct((nb * Ls, B_WIDTH), BF16),
            jax.ShapeDtypeStruct((nb, B_HEADS, B_HEAD_DIM, B_HEAD_DIM), F32),
        ],
        compiler_params=_params("parallel", "parallel"),
        name="ret_sample",
    )(log_gamma, h, h, h, h, tab_b, state_b)


def _cmlp_kernel(u0_ref, u1_ref, v0_ref, v1_ref, g_ref, b_ref, ws_ref, wbt_ref, o_ref, *maybe_vn_ref):
    R = u0_ref.shape[0]
    u = _gelu(jnp.concatenate([u0_ref[...], u1_ref[...]], axis=1))
    v = jnp.concatenate([v0_ref[...], v1_ref[...]], axis=1)
    vn = _layer_norm(_gelu(v), g_ref[...], b_ref[...])
    if maybe_vn_ref:
        maybe_vn_ref[0][...] = vn
    vb = vn.astype(BF16)
    ii = lax.broadcasted_iota(jnp.int32, (R, R), 0) // CHUNK
    jj = lax.broadcasted_iota(jnp.int32, (R, R), 1) // CHUNK
    for gi in range(C_GROUPS):
        wm = jnp.where(jj <= ii, ws_ref[gi][:R, :R], 0.0).astype(BF16)
        sl = slice(gi * C_GROUP_DIM, (gi + 1) * C_GROUP_DIM)
        sv = jnp.dot(wm, vb[:, sl], preferred_element_type=F32) + wbt_ref[:R, gi:gi + 1]
        o_ref[:, sl] = (u[:, sl] * sv).astype(o_ref.dtype)


def cmlp(h, c_ln_g3, c_ln_b3, c_ws, c_wbt, l, row0, n_blocks, R, with_vn):
    b0 = row0 // R
    hw = C_WIDTH // 2
    out_shape = [jax.ShapeDtypeStruct((n_blocks * R, C_WIDTH), BF16)]
    out_specs = [pl.BlockSpec((R, C_WIDTH), lambda i: (i, 0))]
    if with_vn:
        out_shape.append(jax.ShapeDtypeStruct((n_blocks * R, C_WIDTH), F32))
        out_specs.append(pl.BlockSpec((R, C_WIDTH), lambda i: (i, 0)))
    return pl.pallas_call(
        _cmlp_kernel,
        grid=(n_blocks,),
        in_specs=[
            pl.BlockSpec((R, hw), lambda i: (b0 + i, OFF_CU // hw)),
            pl.BlockSpec((R, hw), lambda i: (b0 + i, OFF_CU // hw + 1)),
            pl.BlockSpec((R, hw), lambda i: (b0 + i, OFF_CV // hw)),
            pl.BlockSpec((R, hw), lambda i: (b0 + i, OFF_CV // hw + 1)),
            pl.BlockSpec((None, 1, C_WIDTH), lambda i: (l, 0, 0)),
            pl.BlockSpec((None, 1, C_WIDTH), lambda i: (l, 0, 0)),
            pl.BlockSpec((None, C_GROUPS, CMLP_CHUNK, CMLP_CHUNK), lambda i: (l, 0, 0, 0)),
            pl.BlockSpec((None, CMLP_CHUNK, C_GROUPS), lambda i: (l, 0, 0)),
        ],
        out_specs=out_specs,
        out_shape=out_shape,
        compiler_params=_params("parallel"),
        name="cmlp_vn" if with_vn else "cmlp",
    )(h, h, h, h, c_ln_g3, c_ln_b3, c_ws, c_wbt)


def _split_bf16(x):
    hi = x.astype(BF16)
    lo = (x - hi.astype(F32)).astype(BF16)
    return hi, lo


def _norm1_router_kernel(x_ref, h_ref, g_ref, b_ref, rw_ref, rb_ref,
                         x1_ref, gates_ref, idx_ref, p_ref, rank_ref, counts_ref, cnt_scr, *, alpha):
    @pl.when(pl.program_id(0) == 0)
    def _():
        cnt_scr[...] = jnp.zeros_like(cnt_scr)

    x1 = _layer_norm(alpha * x_ref[...] + h_ref[...], g_ref[...], b_ref[...])
    x1_ref[...] = x1
    xh, xl = _split_bf16(x1)
    wh, wl = _split_bf16(rw_ref[...])
    dot = functools.partial(jnp.dot, preferred_element_type=F32)
    logits = dot(xh, wh) + (dot(xh, wl) + dot(xl, wh)) + rb_ref[...]
    R = logits.shape[0]
    lane = lax.broadcasted_iota(jnp.int32, (R, N_EXPERTS), 1).astype(F32)
    lane_k = lax.broadcasted_iota(jnp.int32, (R, TOP_K), 1)
    work = logits
    vals, idxs = [], []
    for _ in range(TOP_K):
        m = jnp.max(work, -1, keepdims=True)
        ix = jnp.min(jnp.where(work == m, lane, float(N_EXPERTS)), -1, keepdims=True)
        vals.append(m)
        idxs.append(ix)
        work = jnp.where(lane == ix, -jnp.inf, work)
    es = [jnp.exp(v - vals[0]) for v in vals]
    tot = es[0] + es[1] + es[2] + es[3]
    gates = jnp.zeros((R, N_EXPERTS), F32)
    idx_out = jnp.zeros((R, TOP_K), jnp.int32)
    p_out = jnp.zeros((R, TOP_K), F32)
    for k in range(TOP_K):
        pk = es[k] / tot
        gates = jnp.where(lane == idxs[k], pk, gates)
        idx_out = jnp.where(lane_k == k, idxs[k].astype(jnp.int32), idx_out)
        p_out = jnp.where(lane_k == k, pk, p_out)
    gates_ref[...] = gates
    idx_ref[...] = idx_out
    p_ref[...] = p_out
    onehot = jnp.zeros((R, N_EXPERTS), F32)
    for k in range(TOP_K):
        onehot = jnp.where(lane == idxs[k], 1.0, onehot)
    before = lax.broadcasted_iota(jnp.int32, (R, R), 0) > lax.broadcasted_iota(jnp.int32, (R, R), 1)
    prefix = jnp.dot(jnp.where(before, 1.0, 0.0).astype(BF16), onehot.astype(BF16), preferred_element_type=F32)
    base = cnt_scr[...] + prefix
    rank_out = jnp.zeros((R, TOP_K), jnp.int32)
    for k in range(TOP_K):
        rk = jnp.sum(jnp.where(lane == idxs[k], base, 0.0), -1, keepdims=True)
        rank_out = jnp.where(lane_k == k, rk.astype(jnp.int32), rank_out)
    rank_ref[...] = rank_out
    cnt_scr[...] += jnp.sum(onehot, 0, keepdims=True)
    counts_ref[...] = cnt_scr[...]


def norm1_router(x, h2, ln_g3, ln_b3, router_w, router_b3, l, alpha):
    T = x.shape[0]
    tm = _row_tile(T, 256)
    row = lambda i: (i, 0)
    vec = lambda i: (l, 0, 0)
    return pl.pallas_call(
        functools.partial(_norm1_router_kernel, alpha=alpha),
        grid=(T // tm,),
        in_specs=[
            pl.BlockSpec((tm, D_MODEL), row),
            pl.BlockSpec((tm, D_MODEL), row),
            pl.BlockSpec((None, 1, D_MODEL), vec),
            pl.BlockSpec((None, 1, D_MODEL), vec),
            pl.BlockSpec((None, D_MODEL, N_EXPERTS), vec),
            pl.BlockSpec((None, 1, N_EXPERTS), vec),
        ],
        out_specs=[
            pl.BlockSpec((tm, D_MODEL), row),
            pl.BlockSpec((tm, N_EXPERTS), row),
            pl.BlockSpec((tm, TOP_K), row),
            pl.BlockSpec((tm, TOP_K), row),
            pl.BlockSpec((tm, TOP_K), row),
            pl.BlockSpec((1, N_EXPERTS), lambda i: (0, 0)),
        ],
        out_shape=[
            jax.ShapeDtypeStruct((T, D_MODEL), F32),
            jax.ShapeDtypeStruct((T, N_EXPERTS), F32),
            jax.ShapeDtypeStruct((T, TOP_K), jnp.int32),
            jax.ShapeDtypeStruct((T, TOP_K), F32),
            jax.ShapeDtypeStruct((T, TOP_K), jnp.int32),
            jax.ShapeDtypeStruct((1, N_EXPERTS), F32),
        ],
        scratch_shapes=[pltpu.VMEM((1, N_EXPERTS), F32)],
        compiler_params=_params("arbitrary"),
        name="norm1_router",
    )(x, h2, ln_g3, ln_b3, router_w, router_b3)


def _moe_tiles(n_tokens):
    n = -(-n_tokens * TOP_K // MOE_TILE) + N_EXPERTS
    return n + n % 2


def routing_plan(idx, rank, counts):
    T = idx.shape[0]
    n_tiles = _moe_tiles(T)
    counts = counts.reshape(N_EXPERTS).astype(jnp.int32)
    padded = ((counts + MOE_TILE - 1) // MOE_TILE) * MOE_TILE
    pend = jnp.cumsum(padded)
    pstart = pend - padded
    experts = jnp.arange(N_EXPERTS, dtype=jnp.int32)
    dest = jnp.sum(jnp.where(idx[:, :, None] == experts, pstart, 0), -1) + rank
    tile_start = jnp.arange(n_tiles, dtype=jnp.int32) * MOE_TILE
    tile_e = jnp.minimum(jnp.sum((pend[None, :] <= tile_start[:, None]).astype(jnp.int32), 1), N_EXPERTS - 1)
    n_used = pend[-1] // MOE_TILE
    pad_row = jnp.where(padded > 0, pend - MOE_TILE, -1).astype(jnp.int32)
    tile = jnp.arange(n_tiles, dtype=jnp.int32)
    prev_e = jnp.concatenate([jnp.full((1,), -1, jnp.int32), tile_e[:-1]])
    first = ((tile < n_used) & (tile_e != prev_e)).astype(jnp.int32)
    run_end = jnp.sum(jnp.where(tile_e[:, None] == experts, pend // MOE_TILE, 0), -1)
    wraps = (run_end >= n_used).astype(jnp.int32)
    nxt_tile = jnp.where(wraps == 1, 0, jnp.minimum(run_end, n_tiles - 1))
    nxt_e = jnp.sum(jnp.where(nxt_tile[:, None] == tile[None, :], tile_e[None, :], 0), -1)
    plan = (tile_e, n_used.astype(jnp.int32).reshape(1), first, nxt_e.astype(jnp.int32), wraps)
    return dest, pad_row, plan


def _dma_rows(n_rows, body):
    unroll = 8
    assert n_rows % unroll == 0

    def trip(t, carry):
        for u in range(unroll):
            body(t * unroll + u)
        return carry

    lax.fori_loop(0, n_rows // unroll, trip, 0)


def _pack_bf16_pairs(x):
    c = x.shape[1] // 2
    lo = pltpu.bitcast(x[:, :c].astype(BF16).astype(F32), jnp.uint32)
    hi = pltpu.bitcast(x[:, c:].astype(BF16).astype(F32), jnp.uint32)
    return (lo >> 16) | (hi & jnp.uint32(0xFFFF0000))


def _unpack_bf16_pairs(p):
    lo = pltpu.bitcast(p << 16, F32)
    hi = pltpu.bitcast(p & jnp.uint32(0xFFFF0000), F32)
    return jnp.concatenate([lo, hi], axis=1)


def _dispatch_kernel(pad_ref, nu_ref, dest_ref, x_ref, xs_hbm, zeros_buf, packed_buf, zero_sem, row_sem):
    tm = x_ref.shape[0]
    n_tiles = xs_hbm.shape[0] // MOE_TILE

    def zero_copy(row):
        start = pl.multiple_of(row, MOE_TILE)
        return pltpu.make_async_copy(zeros_buf, xs_hbm.at[pl.ds(start, MOE_TILE), :], zero_sem)

    @pl.when(pl.program_id(0) == 0)
    def _():
        zeros_buf[...] = jnp.zeros_like(zeros_buf)
        for e in range(N_EXPERTS):
            @pl.when(pad_ref[e] >= 0)
            def _():
                zero_copy(pad_ref[e]).start()

        def start_tail(t, c):
            zero_copy(t * MOE_TILE).start()
            return c

        def wait_tail(t, c):
            zero_copy(t * MOE_TILE).wait()
            return c

        lax.fori_loop(nu_ref[0], n_tiles, start_tail, 0)
        for e in range(N_EXPERTS):
            @pl.when(pad_ref[e] >= 0)
            def _():
                zero_copy(pad_ref[e]).wait()
        lax.fori_loop(nu_ref[0], n_tiles, wait_tail, 0)

    i = pl.program_id(0)
    slot = i % 2
    packed_buf[slot] = _pack_bf16_pairs(x_ref[...])

    def start_row(r):
        for k in range(TOP_K):
            d = dest_ref[0, r * TOP_K + k]
            pltpu.make_async_copy(packed_buf.at[slot, pl.ds(r, 1), :], xs_hbm.at[pl.ds(d, 1), :],
                                  row_sem.at[slot]).start()

    _dma_rows(tm, start_row)

    def wait_rows(s):
        for k in range(TOP_K):
            pltpu.make_async_copy(packed_buf.at[s], xs_hbm.at[pl.ds(0, tm), :], row_sem.at[s]).wait()

    @pl.when(i > 0)
    def _():
        wait_rows(1 - slot)

    @pl.when(i == pl.num_programs(0) - 1)
    def _():
        wait_rows(slot)


def dispatch(x1, dest, pad_row, n_used):
    T = x1.shape[0]
    tm = _row_tile(T, 256)
    n_rows = _moe_tiles(T) * MOE_TILE
    dest3 = dest.reshape(T // tm, 1, tm * TOP_K)
    return pl.pallas_call(
        _dispatch_kernel,
        grid_spec=pltpu.PrefetchScalarGridSpec(
            num_scalar_prefetch=2,
            grid=(T // tm,),
            in_specs=[
                pl.BlockSpec((None, 1, tm * TOP_K), lambda i, *_: (i, 0, 0), memory_space=pltpu.SMEM),
                pl.BlockSpec((tm, D_MODEL), lambda i, *_: (i, 0)),
            ],
            out_specs=pl.BlockSpec(memory_space=pl.ANY),
            scratch_shapes=[
                pltpu.VMEM((MOE_TILE, D_MODEL // 2), jnp.uint32),
                pltpu.VMEM((2, tm, D_MODEL // 2), jnp.uint32),
                pltpu.SemaphoreType.DMA(()),
                pltpu.SemaphoreType.DMA((2,)),
            ],
        ),
        out_shape=jax.ShapeDtypeStruct((n_rows, D_MODEL // 2), jnp.uint32),
        compiler_params=_params("arbitrary"),
        name="dispatch",
    )(pad_row, n_used, dest3, x1)


def _switch_weights(first, very_first, own_e, own_pass, nxt_e, nxt_pass, has_next, slot_ref, copies):
    @pl.when(very_first)
    def _():
        slot_ref[0] = 1
        for c in copies(own_e, own_pass, 0):
            c.start()

    @pl.when(first)
    def _():
        slot = 1 - slot_ref[0]
        slot_ref[0] = slot
        for c in copies(own_e, own_pass, slot):
            c.wait()

        @pl.when(has_next)
        def _():
            for c in copies(nxt_e, nxt_pass, 1 - slot):
                c.start()


def _pair_step(t0, nu, first_ref, rows, switch, compute, clear):
    t1 = t0 + 1

    @pl.when(t0 < nu)
    def _():
        switch(t0)
        same_run = (t1 < nu) & (first_ref[t1] == 0)

        @pl.when(same_run)
        def _():
            compute(0, 2 * rows, t0)

        @pl.when(jnp.logical_not(same_run))
        def _():
            compute(0, rows, t0)

            @pl.when(t1 < nu)
            def _():
                switch(t1)
                compute(rows, rows, t1)

            @pl.when(t1 >= nu)
            def _():
                clear(rows, rows)

    @pl.when(t0 >= nu)
    def _():
        clear(0, 2 * rows)


def _expert_up_kernel(te_ref, nu_ref, first_ref, nxt_ref, wraps_ref, x_ref, wgu_hbm, bg_ref, bu_ref, a_ref,
                      wbuf, wsem, slot_ref, *, l, tn):
    j = pl.program_id(0)
    nj = pl.num_programs(0)

    def copies(e, jj, slot):
        cols = lambda part: pl.ds(pl.multiple_of(part * EXPERT_DIM + jj * tn, tn), tn)
        return [pltpu.make_async_copy(wgu_hbm.at[l, e, :, cols(part)], wbuf.at[slot, part], wsem.at[slot, part])
                for part in range(2)]

    def switch(t):
        _switch_weights(first_ref[t] == 1, (j == 0) & (t == 0), te_ref[t], j,
                        nxt_ref[t], j + wraps_ref[t], j + wraps_ref[t] < nj, slot_ref, copies)

    def compute(r0, n, t):
        slot = slot_ref[0]
        e = te_ref[t]
        x = _unpack_bf16_pairs(x_ref[r0:r0 + n, :]).astype(BF16)
        hg = jnp.dot(x, wbuf[slot, 0].astype(BF16), preferred_element_type=F32) + bg_ref[pl.ds(e, 1), :]
        hu = jnp.dot(x, wbuf[slot, 1].astype(BF16), preferred_element_type=F32) + bu_ref[pl.ds(e, 1), :]
        g = jnp.minimum(hg, SWIGLU_LIMIT)
        u = jnp.clip(hu, -SWIGLU_LIMIT, SWIGLU_LIMIT)
        a_ref[r0:r0 + n, :] = ((u + 1.0) * g * _sigmoid(SWIGLU_ALPHA * g)).astype(a_ref.dtype)

    def clear(r0, n):
        a_ref[r0:r0 + n, :] = jnp.zeros((n, a_ref.shape[1]), a_ref.dtype)

    _pair_step(2 * pl.program_id(1), nu_ref[0], first_ref, MOE_TILE, switch, compute, clear)


def expert_up(xs, exp_w_gu, exp_b_gu, plan, l):
    tile_e, n_used, first, nxt_e, wraps = plan
    n_rows = xs.shape[0]
    pair = 2 * MOE_TILE
    tn = 512
    nj = EXPERT_DIM // tn
    pc = lambda m, nu: jnp.minimum(m, (nu[0] + 1) // 2 - 1)
    return pl.pallas_call(
        functools.partial(_expert_up_kernel, l=l, tn=tn),
        grid_spec=pltpu.PrefetchScalarGridSpec(
            num_scalar_prefetch=5,
            grid=(nj, n_rows // pair),
            in_specs=[
                pl.BlockSpec((pair, D_MODEL // 2), lambda j, m, te, nu, *_: (pc(m, nu), 0)),
                pl.BlockSpec(memory_space=pl.ANY),
                pl.BlockSpec((None, N_EXPERTS, tn), lambda j, m, *_: (l, 0, j)),
                pl.BlockSpec((None, N_EXPERTS, tn), lambda j, m, *_: (l, 0, nj + j)),
            ],
            out_specs=pl.BlockSpec((pair, tn), lambda j, m, *_: (m, j)),
            scratch_shapes=[
                pltpu.VMEM((2, 2, D_MODEL, tn), F32),
                pltpu.SemaphoreType.DMA((2, 2)),
                pltpu.SMEM((1,), jnp.int32),
            ],
        ),
        out_shape=jax.ShapeDtypeStruct((n_rows, EXPERT_DIM), BF16),
        compiler_params=_params("arbitrary", "arbitrary"),
        name="expert_up",
    )(tile_e, n_used, first, nxt_e, wraps, xs, exp_w_gu, exp_b_gu, exp_b_gu)


def _expert_down_kernel(te_ref, nu_ref, first_ref, nxt_ref, wraps_ref, a_ref, wd_hbm, y_ref,
                        wbuf, wsem, slot_ref, *, l):
    def copies(e, _, slot):
        return [pltpu.make_async_copy(wd_hbm.at[l, e], wbuf.at[slot], wsem.at[slot])]

    def switch(t):
        _switch_weights(first_ref[t] == 1, t == 0, te_ref[t], 0, nxt_ref[t], 0, wraps_ref[t] == 0,
                        slot_ref, copies)

    def compute(r0, n, t):
        y = jnp.dot(a_ref[r0:r0 + n, :], wbuf[slot_ref[0]].astype(BF16), preferred_element_type=F32)
        y_ref[r0:r0 + n, :] = _pack_bf16_pairs(y)

    def clear(r0, n):
        y_ref[r0:r0 + n, :] = jnp.zeros((n, y_ref.shape[1]), y_ref.dtype)

    _pair_step(2 * pl.program_id(0), nu_ref[0], first_ref, MOE_TILE, switch, compute, clear)


def expert_down(a, exp_w_d, plan, l):
    tile_e, n_used, first, nxt_e, wraps = plan
    n_rows = a.shape[0]
    pair = 2 * MOE_TILE
    pc = lambda m, nu: jnp.minimum(m, (nu[0] + 1) // 2 - 1)
    return pl.pallas_call(
        functools.partial(_expert_down_kernel, l=l),
        grid_spec=pltpu.PrefetchScalarGridSpec(
            num_scalar_prefetch=5,
            grid=(n_rows // pair,),
            in_specs=[
                pl.BlockSpec((pair, EXPERT_DIM), lambda m, te, nu, *_: (pc(m, nu), 0)),
                pl.BlockSpec(memory_space=pl.ANY),
            ],
            out_specs=pl.BlockSpec((pair, D_MODEL // 2), lambda m, *_: (m, 0)),
            scratch_shapes=[
                pltpu.VMEM((2, EXPERT_DIM, D_MODEL), F32),
                pltpu.SemaphoreType.DMA((2,)),
                pltpu.SMEM((1,), jnp.int32),
            ],
        ),
        out_shape=jax.ShapeDtypeStruct((n_rows, D_MODEL // 2), jnp.uint32),
        compiler_params=_params("arbitrary"),
        name="expert_down",
    )(tile_e, n_used, first, nxt_e, wraps, a, exp_w_d)


COMBINE_LOOKAHEAD = 2


def _combine_norm2_kernel(d0_ref, d1_ref, d2_ref, x1_ref, ys_hbm, p_ref, gates_ref, bd_ref, g_ref, b_ref,
                          x2_ref, x2b_ref, rows_a, rows_b, rows_c, sems, *, alpha):
    i = pl.program_id(0)
    n = pl.num_programs(0)
    tm = x1_ref.shape[0]
    n_rows = TOP_K * tm
    bufs = (rows_a, rows_b, rows_c)

    def row_copy(d_ref, r, s):
        return pltpu.make_async_copy(ys_hbm.at[pl.ds(d_ref[0, r], 1), :], bufs[s].at[pl.ds(r, 1), :], sems.at[s])

    def wait_rows(s):
        pltpu.make_async_copy(ys_hbm.at[pl.ds(0, n_rows), :], bufs[s], sems.at[s]).wait()

    @pl.when(i == 0)
    def _():
        _dma_rows(n_rows, lambda r: row_copy(d0_ref, r, 0).start())
        _dma_rows(n_rows, lambda r: row_copy(d1_ref, r, 1).start())

    def step(cur_s):
        ahead_s = (cur_s + COMBINE_LOOKAHEAD) % 3
        cur = bufs[cur_s]
        wait_rows(cur_s)
        for r in range(n_rows):
            row_copy(d2_ref, r, ahead_s).start()
        p = p_ref[...]
        y = jnp.dot(gates_ref[...].astype(BF16), bd_ref[...].astype(BF16), preferred_element_type=F32)
        for k in range(TOP_K):
            y += p[:, k:k + 1] * _unpack_bf16_pairs(cur[k * tm:(k + 1) * tm, :])
        x2 = _layer_norm(alpha * x1_ref[...] + y, g_ref[...], b_ref[...])
        x2_ref[...] = x2
        x2b_ref[...] = x2.astype(BF16)

        @pl.when(i == n - 1)
        def _():
            wait_rows((cur_s + 1) % 3)
            wait_rows(ahead_s)

    for s in range(3):
        @pl.when(i % 3 == s)
        def _(s=s):
            step(s)


def combine_norm2(x1, ys, dest, p, gates, exp_b_d, ln_g3, ln_b3, l, alpha):
    T = x1.shape[0]
    tm = _row_tile(T, 128)
    nt = T // tm
    row = lambda i: (i, 0)
    vec = lambda i: (l, 0, 0)
    dest3 = dest.reshape(nt, tm, TOP_K).transpose(0, 2, 1).reshape(nt, 1, TOP_K * tm)
    return pl.pallas_call(
        functools.partial(_combine_norm2_kernel, alpha=alpha),
        grid=(nt,),
        in_specs=[
            pl.BlockSpec((None, 1, TOP_K * tm), lambda i: (i, 0, 0), memory_space=pltpu.SMEM),
            pl.BlockSpec((None, 1, TOP_K * tm), lambda i: (jnp.minimum(i + 1, nt - 1), 0, 0),
                         memory_space=pltpu.SMEM),
            pl.BlockSpec((None, 1, TOP_K * tm), lambda i: (jnp.minimum(i + COMBINE_LOOKAHEAD, nt - 1), 0, 0),
                         memory_space=pltpu.SMEM),
            pl.BlockSpec((tm, D_MODEL), row),
            pl.BlockSpec(memory_space=pl.ANY),
            pl.BlockSpec((tm, TOP_K), row),
            pl.BlockSpec((tm, N_EXPERTS), row),
            pl.BlockSpec((None, N_EXPERTS, D_MODEL), vec),
            pl.BlockSpec((None, 1, D_MODEL), vec),
            pl.BlockSpec((None, 1, D_MODEL), vec),
        ],
        out_specs=[pl.BlockSpec((tm, D_MODEL), row), pl.BlockSpec((tm, D_MODEL), row)],
        out_shape=[jax.ShapeDtypeStruct((T, D_MODEL), F32), jax.ShapeDtypeStruct((T, D_MODEL), BF16)],
        scratch_shapes=[
            pltpu.VMEM((TOP_K * tm, D_MODEL // 2), jnp.uint32),
            pltpu.VMEM((TOP_K * tm, D_MODEL // 2), jnp.uint32),
            pltpu.VMEM((TOP_K * tm, D_MODEL // 2), jnp.uint32),
            pltpu.SemaphoreType.DMA((3,)),
        ],
        compiler_params=_params("arbitrary"),
        name="combine_norm2",
    )(dest3, dest3, dest3, x1, ys, p, gates, exp_b_d, ln_g3, ln_b3)


def _ple_kernel(xb_ref, x_ref, p_ref, wg_ref, wp_ref, o_ref, ob_ref):
    wg = wg_ref[...].astype(BF16)
    wp = wp_ref[...].astype(BF16)
    parts = 2
    rows = x_ref.shape[0] // parts
    for part in range(parts):
        sl = pl.ds(part * rows, rows)
        gate = _sigmoid(jnp.dot(xb_ref[sl, :], wg, preferred_element_type=F32))
        pe = jnp.dot(p_ref[sl, :].astype(BF16), wp, preferred_element_type=F32)
        out = x_ref[sl, :] + gate * pe
        o_ref[sl, :] = out
        ob_ref[sl, :] = out.astype(BF16)


def ple(x2, x2b, p_l, ple_gate_w, ple_w, l):
    T = x2.shape[0]
    tm = _row_tile(T, 1088)
    tn = 512
    return pl.pallas_call(
        _ple_kernel,
        grid=(T // tm, D_MODEL // tn),
        in_specs=[
            pl.BlockSpec((tm, D_MODEL), lambda i, j: (i, 0)),
            pl.BlockSpec((tm, tn), lambda i, j: (i, j)),
            pl.BlockSpec((tm, PLE_DIM), lambda i, j: (i, 0)),
            pl.BlockSpec((None, D_MODEL, tn), lambda i, j: (l, 0, j)),
            pl.BlockSpec((None, PLE_DIM, tn), lambda i, j: (l, 0, j)),
        ],
        out_specs=[pl.BlockSpec((tm, tn), lambda i, j: (i, j)), pl.BlockSpec((tm, tn), lambda i, j: (i, j))],
        out_shape=[jax.ShapeDtypeStruct((T, D_MODEL), F32), jax.ShapeDtypeStruct((T, D_MODEL), BF16)],
        compiler_params=_params("parallel", "parallel"),
        name="ple",
    )(x2b, x2, p_l, ple_gate_w, ple_w)


def _tables(pos):
    posf = pos.astype(F32)[:, None]
    half = ROT_DIM_A // 2
    inv = jnp.float32(ROPE_THETA_A) ** (-jnp.arange(half, dtype=F32) / half)
    ang = posf * inv[None, :]
    cos, sin = jnp.cos(ang), jnp.sin(ang)
    n = pos.shape[0]
    ones = jnp.ones((n, HEAD_DIM_A - ROT_DIM_A), F32)
    zeros = jnp.zeros((n, HEAD_DIM_A - ROT_DIM_A), F32)
    z8 = jnp.zeros((n, half), F32)
    c64 = jnp.concatenate([cos, cos, ones], 1)
    lo64 = jnp.concatenate([-sin, z8, zeros], 1)
    hi64 = jnp.concatenate([z8, sin, zeros], 1)
    tab_a = jnp.concatenate([c64, c64, lo64, lo64, hi64, hi64], 1)
    hb = B_HEAD_DIM // 2
    invb = jnp.float32(RET_THETA) ** (-jnp.arange(hb, dtype=F32) / hb)
    angb = posf * invb[None, :]
    tab_b = jnp.concatenate([jnp.cos(angb), jnp.sin(angb)], 1)
    return tab_a, tab_b


def kernel(x_prompt, x_sample, cache_a_k, cache_a_v, state_b, p_prompt, p_sample, w_in, b_in, a_sinks, c_ln_g, c_ln_b, c_ws, c_wb, w_out, b_out, ln1_g, ln1_b, router_w, router_b, exp_w_gu, exp_b_gu, exp_w_d, exp_b_d, ln2_g, ln2_b, ple_w, ple_gate_w):
    depth = w_in.shape[0]
    alpha = (2 * depth) ** 0.25
    bp, S, _ = x_prompt.shape
    nb, Ls, _ = x_sample.shape
    assert bp == 1 and Ls == CHUNK and S % RET_BLOCK == 0
    win = cache_a_k.shape[2]
    n_s = nb * Ls

    log_gamma = jnp.log(1.0 - 2.0 ** (-5.0 - jnp.arange(B_HEADS, dtype=F32)))
    pos = jnp.concatenate([jnp.arange(S), jnp.tile(PAST_LEN + jnp.arange(Ls), nb)])
    tab_a, tab_b = _tables(pos)

    x = jnp.concatenate([x_prompt.reshape(S, D_MODEL), x_sample.reshape(n_s, D_MODEL)], 0)
    xb = x.astype(BF16)
    cache_k = cache_a_k.reshape(depth, nb, win, A_KV_WIDTH)
    cache_v = cache_a_v.reshape(depth, nb, win, A_KV_WIDTH)
    vec3 = lambda a: a.reshape(depth, 1, a.shape[-1])
    b_in3, b_out3 = vec3(b_in), vec3(b_out)
    c_ln_g3, c_ln_b3 = vec3(c_ln_g), vec3(c_ln_b)
    ln1_g3, ln1_b3, ln2_g3, ln2_b3 = vec3(ln1_g), vec3(ln1_b), vec3(ln2_g), vec3(ln2_b)
    router_b3 = vec3(router_b)
    c_wbt = jnp.swapaxes(c_wb, 1, 2)

    ak_p, av_p, rb_p, ak_s, av_s, rb_s, cv_s = [], [], [], [], [], [], []
    for l in range(depth):
        h = in_proj(xb, w_in, b_in3, l)
        oa_p, kr_p = attn_prompt(h, tab_a, a_sinks, l, S)
        oa_s, kr_s = attn_sample(h, tab_a, a_sinks, cache_k, cache_v, l, S, nb, Ls)
        ob_p, r_p = ret_prompt(h, tab_b, log_gamma, S)
        ob_s, r_s = ret_sample(h, tab_b, log_gamma, state_b, l, S, nb, Ls)
        (oc_p,) = cmlp(h, c_ln_g3, c_ln_b3, c_ws, c_wbt, l, 0, S // CMLP_CHUNK, CMLP_CHUNK, False)
        oc_s, vn_s = cmlp(h, c_ln_g3, c_ln_b3, c_ws, c_wbt, l, S, nb, Ls, True)
        oa = jnp.concatenate([oa_p, oa_s], 0)
        ob = jnp.concatenate([ob_p, ob_s], 0)
        oc = jnp.concatenate([oc_p, oc_s], 0)
        h2 = out_proj(oa, ob, oc, w_out, b_out3, l)
        x1, gates, idx, p, rank, counts = norm1_router(x, h2, ln1_g3, ln1_b3, router_w, router_b3, l, alpha)

        dest, pad_row, plan = routing_plan(idx, rank, counts)
        xs = dispatch(x1, dest, pad_row, plan[1])
        a = expert_up(xs, exp_w_gu, exp_b_gu, plan, l)
        ys = expert_down(a, exp_w_d, plan, l)
        x2, x2b = combine_norm2(x1, ys, dest, p, gates, exp_b_d, ln2_g3, ln2_b3, l, alpha)

        p_l = jnp.concatenate([p_prompt[l].reshape(S, PLE_DIM), p_sample[l].reshape(n_s, PLE_DIM)], 0)
        x, xb = ple(x2, x2b, p_l, ple_gate_w, ple_w, l)

        ak_p.append(kr_p[S - win:].reshape(1, win, A_KV_HEADS, HEAD_DIM_A))
        av_p.append(h[S - win:S, OFF_AV:OFF_AV + A_KV_WIDTH].reshape(1, win, A_KV_HEADS, HEAD_DIM_A))
        rb_p.append(r_p[None])
        k_all = jnp.concatenate([cache_k[l], kr_s.reshape(nb, Ls, A_KV_WIDTH)], 1)
        v_all = jnp.concatenate([cache_v[l], h[S:, OFF_AV:OFF_AV + A_KV_WIDTH].reshape(nb, Ls, A_KV_WIDTH)], 1)
        ak_s.append(k_all[:, -win:].reshape(nb, win, A_KV_HEADS, HEAD_DIM_A))
        av_s.append(v_all[:, -win:].reshape(nb, win, A_KV_HEADS, HEAD_DIM_A))
        rb_s.append(r_s)
        cv_s.append(vn_s.reshape(nb, Ls, C_WIDTH))

    return (x[:S].reshape(1, S, D_MODEL), x[S:].reshape(nb, Ls, D_MODEL),
            jnp.stack(ak_p), jnp.stack(av_p), jnp.stack(rb_p),
            jnp.stack(ak_s), jnp.stack(av_s), jnp.stack(rb_s), jnp.stack(cv_s))
```

```python
import functools

import jax
import jax.numpy as jnp
from jax import lax
from jax.experimental import pallas as pl
from jax.experimental.pallas import tpu as pltpu

D_MODEL = 4096
CHUNK = 64
PAST_LEN = 4096
HEAD_DIM_A = 64
A_WIDTH = 1024
A_KV_HEADS = 4
A_GROUP = 4
A_KV_WIDTH = 256
ROT_DIM_A = 16
ROPE_THETA_A = 500000.0
B_HEAD_DIM = 256
B_WIDTH = 2048
B_HEADS = 8
RET_THETA = 10000.0
RET_BLOCK = 256
C_WIDTH = 1024
C_GROUPS = 4
C_GROUP_DIM = 256
CMLP_CHUNK = 128
OFF_AQ, OFF_AK, OFF_AV = 0, 1024, 1280
OFF_BQ, OFF_BK, OFF_BV, OFF_BG = 1536, 3584, 5632, 7680
OFF_CU, OFF_CV = 9728, 10752
IN_COLS = 11776
N_EXPERTS = 32
TOP_K = 4
EXPERT_DIM = 1024
SWIGLU_LIMIT = 7.0
SWIGLU_ALPHA = 1.702
PLE_DIM = 256
LN_EPS = 1e-5
RMS_EPS = 1e-6

LANES = 128
MOE_TILE = 256
VMEM_LIMIT = 56 * 1024 * 1024

BF16 = jnp.bfloat16
F32 = jnp.float32


def _params(*sem):
    return pltpu.CompilerParams(dimension_semantics=sem, vmem_limit_bytes=VMEM_LIMIT)


def _row_tile(n, target):
    best = None
    for t in range(16, min(n, target) + 1, 16):
        if n % t == 0:
            best = t
    assert best is not None, n
    return best


def _layer_norm(x, g, b):
    mu = jnp.mean(x, -1, keepdims=True)
    xc = x - mu
    var = jnp.mean(xc * xc, -1, keepdims=True)
    return xc * lax.rsqrt(var + LN_EPS) * g + b


def _gelu(x):
    c = 0.7978845608028654
    return 0.5 * x * (1.0 + jnp.tanh(c * (x + 0.044715 * (x * x * x))))


def _sigmoid(x):
    return 1.0 / (1.0 + jnp.exp(-x))


def _in_proj_kernel(x_ref, w_ref, b_ref, o_ref):
    w = w_ref[...].astype(BF16)
    o_ref[...] = jnp.dot(x_ref[...], w, preferred_element_type=F32) + b_ref[...]


def in_proj(xb, w_in, b_in3, l):
    T = xb.shape[0]
    n = w_in.shape[2]
    tm = _row_tile(T, 1088)
    tn = 512
    return pl.pallas_call(
        _in_proj_kernel,
        grid=(T // tm, n // tn),
        in_specs=[
            pl.BlockSpec((tm, D_MODEL), lambda i, j: (i, 0)),
            pl.BlockSpec((None, D_MODEL, tn), lambda i, j: (l, 0, j)),
            pl.BlockSpec((None, 1, tn), lambda i, j: (l, 0, j)),
        ],
        out_specs=pl.BlockSpec((tm, tn), lambda i, j: (i, j)),
        out_shape=jax.ShapeDtypeStruct((T, n), F32),
        compiler_params=_params("parallel", "parallel"),
        name="in_proj",
    )(xb, w_in, b_in3)


def _out_proj_kernel(oap_ref, obp_ref, ocp_ref, oas_ref, obs_ref, ocs_ref, w_ref, b_ref, o_ref, wb_ref,
                     *, n_prompt):
    i = pl.program_id(1)

    @pl.when(i == 0)
    def _():
        wb_ref[...] = w_ref[...].astype(BF16)

    def run(oa_ref, ob_ref, oc_ref):
        acc = jnp.dot(oa_ref[...], wb_ref[:A_WIDTH], preferred_element_type=F32)
        acc += jnp.dot(ob_ref[...], wb_ref[A_WIDTH:A_WIDTH + B_WIDTH], preferred_element_type=F32)
        acc += jnp.dot(oc_ref[...], wb_ref[A_WIDTH + B_WIDTH:], preferred_element_type=F32)
        o_ref[...] = acc + b_ref[...]

    @pl.when(i < n_prompt)
    def _():
        run(oap_ref, obp_ref, ocp_ref)

    @pl.when(i >= n_prompt)
    def _():
        run(oas_ref, obs_ref, ocs_ref)


def out_proj(mix_p, mix_s, w_out, b_out3, l):
    S, n_s = mix_p[0].shape[0], mix_s[0].shape[0]
    tm = 512
    while S % tm or n_s % tm:
        tm //= 2
    n_prompt = S // tm
    tn = 512
    prow = lambda j, i: (jnp.minimum(i, n_prompt - 1), 0)
    srow = lambda j, i: (jnp.maximum(i - n_prompt, 0), 0)
    widths = (A_WIDTH, B_WIDTH, C_WIDTH)
    return pl.pallas_call(
        functools.partial(_out_proj_kernel, n_prompt=n_prompt),
        grid=(D_MODEL // tn, (S + n_s) // tm),
        in_specs=[pl.BlockSpec((tm, w), prow) for w in widths] + [pl.BlockSpec((tm, w), srow) for w in widths] + [
            pl.BlockSpec((None, D_MODEL, tn), lambda j, i: (l, 0, j)),
            pl.BlockSpec((None, 1, tn), lambda j, i: (l, 0, j)),
        ],
        out_specs=pl.BlockSpec((tm, tn), lambda j, i: (i, j)),
        out_shape=jax.ShapeDtypeStruct((S + n_s, D_MODEL), F32),
        scratch_shapes=[pltpu.VMEM((D_MODEL, tn), BF16)],
        compiler_params=_params("parallel", "arbitrary"),
        name="out_proj",
    )(*mix_p, *mix_s, w_out, b_out3)


def _rot_a(x, tab):
    c, s_lo, s_hi = tab[:, :LANES], tab[:, LANES:2 * LANES], tab[:, 2 * LANES:]
    half = ROT_DIM_A // 2
    outs = []
    for j in range(x.shape[1] // LANES):
        xc = x[:, j * LANES:(j + 1) * LANES]
        up = pltpu.roll(xc, LANES - half, 1)
        dn = pltpu.roll(xc, half, 1)
        outs.append(xc * c + up * s_lo + dn * s_hi)
    return outs[0] if len(outs) == 1 else jnp.concatenate(outs, axis=1)


def _attend(q, k, v, valid, sink_ref, l, o_ref, acc_ref):
    R = q.shape[0]
    qb = q.astype(BF16)
    kb = k.astype(BF16)
    vb = v.astype(BF16)
    grp = lax.broadcasted_iota(jnp.int32, (A_GROUP * R, 1), 0) // R
    for kh in range(A_KV_HEADS):
        base = kh * A_GROUP
        qs = jnp.concatenate(
            [qb[:, (base + g) * HEAD_DIM_A:(base + g + 1) * HEAD_DIM_A] for g in range(A_GROUP)], axis=0)
        k_h = kb[:, kh * HEAD_DIM_A:(kh + 1) * HEAD_DIM_A]
        v_h = vb[:, kh * HEAD_DIM_A:(kh + 1) * HEAD_DIM_A]
        s = lax.dot_general(qs, k_h, (((1,), (1,)), ((), ())), preferred_element_type=F32)
        if valid is not None:
            s = jnp.where(valid, s, -jnp.inf)
        sink = jnp.zeros((A_GROUP * R, 1), F32)
        for g in range(A_GROUP):
            sink = jnp.where(grp == g, sink_ref[l, base + g], sink)
        m = jnp.maximum(jnp.max(s, -1, keepdims=True), sink)
        e = jnp.exp(s - m)
        denom = jnp.sum(e, -1, keepdims=True) + jnp.exp(sink - m)
        o = jnp.dot(e.astype(BF16), v_h, preferred_element_type=F32) * (1.0 / denom)
        for g in range(A_GROUP):
            h = base + g
            acc_ref[:, h * HEAD_DIM_A:(h + 1) * HEAD_DIM_A] = o[g * R:(g + 1) * R]
    o_ref[...] = acc_ref[...].astype(o_ref.dtype)


def _attn_prompt_kernel(sink_ref, q_ref, kp_ref, kc_ref, vp_ref, vc_ref, tq_ref, tp_ref, o_ref, kr_ref, acc_ref,
                        *, l):
    i = pl.program_id(0)
    R = q_ref.shape[0]
    tq = tq_ref[...]
    q = _rot_a(q_ref[...], tq) * (HEAD_DIM_A ** -0.5)
    kc = _rot_a(kc_ref[...], tq)
    kp = _rot_a(kp_ref[...], tp_ref[...])
    kr_ref[...] = kc
    k = jnp.concatenate([kp, kc], axis=0)
    v = jnp.concatenate([vp_ref[...], vc_ref[...]], axis=0)
    n_q = A_GROUP * R
    qc = (lax.broadcasted_iota(jnp.int32, (n_q, 2 * R), 0) % R) // CHUNK
    col = lax.broadcasted_iota(jnp.int32, (n_q, 2 * R), 1)
    kc_idx = col // CHUNK - R // CHUNK
    lo = jnp.where(i == 0, 0, -(R // CHUNK))
    valid = (kc_idx <= qc) & (kc_idx >= jnp.maximum(qc - 2, lo))
    _attend(q, k, v, valid, sink_ref, l, o_ref, acc_ref)


def attn_prompt(h, tab_a, sinks, l, S):
    R = 2 * CHUNK
    nb = S // R
    prev = lambda i: jnp.maximum(i - 1, 0)
    return pl.pallas_call(
        functools.partial(_attn_prompt_kernel, l=l),
        grid=(nb,),
        in_specs=[
            pl.BlockSpec(memory_space=pltpu.SMEM),
            pl.BlockSpec((R, A_WIDTH), lambda i: (i, 0)),
            pl.BlockSpec((R, A_KV_WIDTH), lambda i: (prev(i), OFF_AK // A_KV_WIDTH)),
            pl.BlockSpec((R, A_KV_WIDTH), lambda i: (i, OFF_AK // A_KV_WIDTH)),
            pl.BlockSpec((R, A_KV_WIDTH), lambda i: (prev(i), OFF_AV // A_KV_WIDTH)),
            pl.BlockSpec((R, A_KV_WIDTH), lambda i: (i, OFF_AV // A_KV_WIDTH)),
            pl.BlockSpec((R, 3 * LANES), lambda i: (i, 0)),
            pl.BlockSpec((R, 3 * LANES), lambda i: (prev(i), 0)),
        ],
        out_specs=[
            pl.BlockSpec((R, A_WIDTH), lambda i: (i, 0)),
            pl.BlockSpec((R, A_KV_WIDTH), lambda i: (i, 0)),
        ],
        out_shape=[
            jax.ShapeDtypeStruct((S, A_WIDTH), BF16),
            jax.ShapeDtypeStruct((S, A_KV_WIDTH), F32),
        ],
        scratch_shapes=[pltpu.VMEM((R, A_WIDTH), F32)],
        compiler_params=_params("parallel"),
        name="attn_prompt",
    )(sinks, h, h, h, h, h, tab_a, tab_a)


def _attn_sample_kernel(sink_ref, q_ref, kn_ref, vn_ref, kcache_ref, vcache_ref, tq_ref, o_ref, kr_ref, acc_ref,
                        *, l):
    tq = tq_ref[...]
    q = _rot_a(q_ref[...], tq) * (HEAD_DIM_A ** -0.5)
    kn = _rot_a(kn_ref[...], tq)
    kr_ref[...] = kn
    k = jnp.concatenate([kcache_ref[...], kn], axis=0)
    v = jnp.concatenate([vcache_ref[...], vn_ref[...]], axis=0)
    _attend(q, k, v, None, sink_ref, l, o_ref, acc_ref)


def attn_sample(h, tab_a, sinks, cache_k, cache_v, l, S, nb, Ls):
    win = cache_k.shape[2]
    r0 = S // Ls
    return pl.pallas_call(
        functools.partial(_attn_sample_kernel, l=l),
        grid=(nb,),
        in_specs=[
            pl.BlockSpec(memory_space=pltpu.SMEM),
            pl.BlockSpec((Ls, A_WIDTH), lambda b: (r0 + b, 0)),
            pl.BlockSpec((Ls, A_KV_WIDTH), lambda b: (r0 + b, OFF_AK // A_KV_WIDTH)),
            pl.BlockSpec((Ls, A_KV_WIDTH), lambda b: (r0 + b, OFF_AV // A_KV_WIDTH)),
            pl.BlockSpec((None, None, win, A_KV_WIDTH), lambda b: (l, b, 0, 0)),
            pl.BlockSpec((None, None, win, A_KV_WIDTH), lambda b: (l, b, 0, 0)),
            pl.BlockSpec((Ls, 3 * LANES), lambda b: (r0 + b, 0)),
        ],
        out_specs=[
            pl.BlockSpec((Ls, A_WIDTH), lambda b: (b, 0)),
            pl.BlockSpec((Ls, A_KV_WIDTH), lambda b: (b, 0)),
        ],
        out_shape=[
            jax.ShapeDtypeStruct((nb * Ls, A_WIDTH), BF16),
            jax.ShapeDtypeStruct((nb * Ls, A_KV_WIDTH), F32),
        ],
        scratch_shapes=[pltpu.VMEM((Ls, A_WIDTH), F32)],
        compiler_params=_params("parallel"),
        name="attn_sample",
    )(sinks, h, h, h, cache_k, cache_v, tab_a)


def _rot_b(x, cs):
    half = B_HEAD_DIM // 2
    c, s = cs[:, :half], cs[:, half:]
    x1, x2 = x[:, :half], x[:, half:]
    return jnp.concatenate([x1 * c - x2 * s, x2 * c + x1 * s], axis=1)


def _retention_step(lg, q, k, v, g, cs, r):
    L = q.shape[0]
    q = _rot_b(q, cs)
    k = _rot_b(k, cs) * (B_HEAD_DIM ** -0.5)
    ii = lax.broadcasted_iota(jnp.int32, (L, L), 0)
    jj = lax.broadcasted_iota(jnp.int32, (L, L), 1)
    diff = (ii - jj).astype(F32)
    decay = jnp.where(diff >= 0, jnp.exp(lg * jnp.maximum(diff, 0.0)), 0.0)
    row = lax.broadcasted_iota(jnp.int32, (L, B_HEAD_DIM), 0).astype(F32)
    qb = q.astype(BF16)
    vb = v.astype(BF16)
    scores = lax.dot_general(qb, k.astype(BF16), (((1,), (1,)), ((), ())), preferred_element_type=F32) * decay
    inner = jnp.dot(scores.astype(BF16), vb, preferred_element_type=F32)
    cross = jnp.dot(qb, r.astype(BF16), preferred_element_type=F32) * jnp.exp(lg * (row + 1.0))
    k_dec = k * jnp.exp(lg * (L - 1.0 - row))
    g_all = jnp.exp(lg * jnp.full((1, B_HEAD_DIM), float(L), F32))
    r_new = r * g_all + jnp.dot(k_dec.T.astype(BF16), vb, preferred_element_type=F32)
    o = inner + cross
    on = o * lax.rsqrt(jnp.mean(o * o, -1, keepdims=True) + RMS_EPS)
    return on * (g * _sigmoid(g)), r_new


RET_HEADS_PER_STEP = 2


def _ret_prompt_kernel(lg_ref, q_ref, k_ref, v_ref, g_ref, cs_ref, o_ref, r_ref, r_scr):
    hb = pl.program_id(0)
    c = pl.program_id(1)

    @pl.when(c == 0)
    def _():
        r_scr[...] = jnp.zeros_like(r_scr)

    cs = cs_ref[...]
    for u in range(RET_HEADS_PER_STEP):
        sl = slice(u * B_HEAD_DIM, (u + 1) * B_HEAD_DIM)
        o, r_new = _retention_step(lg_ref[hb * RET_HEADS_PER_STEP + u], q_ref[:, sl], k_ref[:, sl], v_ref[:, sl],
                                   g_ref[:, sl], cs, r_scr[u])
        o_ref[:, sl] = o.astype(o_ref.dtype)
        r_scr[u] = r_new
        r_ref[u] = r_new


def ret_prompt(h, tab_b, log_gamma, S):
    L = RET_BLOCK
    nc = S // L
    hps = RET_HEADS_PER_STEP
    w = hps * B_HEAD_DIM
    col = lambda off: (lambda hb, c: (c, off // w + hb))
    return pl.pallas_call(
        _ret_prompt_kernel,
        grid=(B_HEADS // hps, nc),
        in_specs=[
            pl.BlockSpec(memory_space=pltpu.SMEM),
            pl.BlockSpec((L, w), col(OFF_BQ)),
            pl.BlockSpec((L, w), col(OFF_BK)),
            pl.BlockSpec((L, w), col(OFF_BV)),
            pl.BlockSpec((L, w), col(OFF_BG)),
            pl.BlockSpec((L, B_HEAD_DIM), lambda hb, c: (c, 0)),
        ],
        out_specs=[
            pl.BlockSpec((L, w), lambda hb, c: (c, hb)),
            pl.BlockSpec((hps, B_HEAD_DIM, B_HEAD_DIM), lambda hb, c: (hb, 0, 0)),
        ],
        out_shape=[
            jax.ShapeDtypeStruct((S, B_WIDTH), BF16),
            jax.ShapeDtypeStruct((B_HEADS, B_HEAD_DIM, B_HEAD_DIM), F32),
        ],
        scratch_shapes=[pltpu.VMEM((hps, B_HEAD_DIM, B_HEAD_DIM), F32)],
        compiler_params=_params("parallel", "arbitrary"),
        name="ret_prompt",
    )(log_gamma, h, h, h, h, tab_b)


def _ret_sample_kernel(lg_ref, q_ref, k_ref, v_ref, g_ref, cs_ref, r0_ref, o_ref, r_ref):
    hb = pl.program_id(1)
    cs = cs_ref[...]
    for u in range(RET_HEADS_PER_STEP):
        sl = slice(u * B_HEAD_DIM, (u + 1) * B_HEAD_DIM)
        o, r_new = _retention_step(lg_ref[hb * RET_HEADS_PER_STEP + u], q_ref[:, sl], k_ref[:, sl], v_ref[:, sl],
                                   g_ref[:, sl], cs, r0_ref[u])
        o_ref[:, sl] = o.astype(o_ref.dtype)
        r_ref[u] = r_new


def ret_sample(h, tab_b, log_gamma, state_b, l, S, nb, Ls):
    r0 = S // Ls
    hps = RET_HEADS_PER_STEP
    w = hps * B_HEAD_DIM
    col = lambda off: (lambda b, hb: (r0 + b, off // w + hb))
    return pl.pallas_call(
        _ret_sample_kernel,
        grid=(nb, B_HEADS // hps),
        in_specs=[
            pl.BlockSpec(memory_space=pltpu.SMEM),
            pl.BlockSpec((Ls, w), col(OFF_BQ)),
            pl.BlockSpec((Ls, w), col(OFF_BK)),
            pl.BlockSpec((Ls, w), col(OFF_BV)),
            pl.BlockSpec((Ls, w), col(OFF_BG)),
            pl.BlockSpec((Ls, B_HEAD_DIM), lambda b, hb: (r0 + b, 0)),
            pl.BlockSpec((None, None, hps, B_HEAD_DIM, B_HEAD_DIM), lambda b, hb: (l, b, hb, 0, 0)),
        ],
        out_specs=[
            pl.BlockSpec((Ls, w), lambda b, hb: (b, hb)),
            pl.BlockSpec((None, hps, B_HEAD_DIM, B_HEAD_DIM), lambda b, hb: (b, hb, 0, 0)),
        ],
        out_shape=[
            jax.ShapeDtypeStruct((nb * Ls, B_WIDTH), BF16),
            jax.ShapeDtypeStruct((nb, B_HEADS, B_HEAD_DIM, B_HEAD_DIM), F32),
        ],
        compiler_params=_params("parallel", "parallel"),
        name="ret_sample",
    )(log_gamma, h, h, h, h, tab_b, state_b)


def _cmlp_kernel(u0_ref, u1_ref, v0_ref, v1_ref, g_ref, b_ref, ws_ref, wbt_ref, o_ref, *maybe_vn_ref):
    R = u0_ref.shape[0]
    u = _gelu(jnp.concatenate([u0_ref[...], u1_ref[...]], axis=1))
    v = jnp.concatenate([v0_ref[...], v1_ref[...]], axis=1)
    vn = _layer_norm(_gelu(v), g_ref[...], b_ref[...])
    if maybe_vn_ref:
        maybe_vn_ref[0][...] = vn
    vb = vn.astype(BF16)
    ii = lax.broadcasted_iota(jnp.int32, (R, R), 0) // CHUNK
    jj = lax.broadcasted_iota(jnp.int32, (R, R), 1) // CHUNK
    for gi in range(C_GROUPS):
        wm = jnp.where(jj <= ii, ws_ref[gi][:R, :R], 0.0).astype(BF16)
        sl = slice(gi * C_GROUP_DIM, (gi + 1) * C_GROUP_DIM)
        sv = jnp.dot(wm, vb[:, sl], preferred_element_type=F32) + wbt_ref[:R, gi:gi + 1]
        o_ref[:, sl] = (u[:, sl] * sv).astype(o_ref.dtype)


def cmlp(h, c_ln_g3, c_ln_b3, c_ws, c_wbt, l, row0, n_blocks, R, with_vn):
    b0 = row0 // R
    hw = C_WIDTH // 2
    out_shape = [jax.ShapeDtypeStruct((n_blocks * R, C_WIDTH), BF16)]
    out_specs = [pl.BlockSpec((R, C_WIDTH), lambda i: (i, 0))]
    if with_vn:
        out_shape.append(jax.ShapeDtypeStruct((n_blocks * R, C_WIDTH), F32))
        out_specs.append(pl.BlockSpec((R, C_WIDTH), lambda i: (i, 0)))
    return pl.pallas_call(
        _cmlp_kernel,
        grid=(n_blocks,),
        in_specs=[
            pl.BlockSpec((R, hw), lambda i: (b0 + i, OFF_CU // hw)),
            pl.BlockSpec((R, hw), lambda i: (b0 + i, OFF_CU // hw + 1)),
            pl.BlockSpec((R, hw), lambda i: (b0 + i, OFF_CV // hw)),
            pl.BlockSpec((R, hw), lambda i: (b0 + i, OFF_CV // hw + 1)),
            pl.BlockSpec((None, 1, C_WIDTH), lambda i: (l, 0, 0)),
            pl.BlockSpec((None, 1, C_WIDTH), lambda i: (l, 0, 0)),
            pl.BlockSpec((None, C_GROUPS, CMLP_CHUNK, CMLP_CHUNK), lambda i: (l, 0, 0, 0)),
            pl.BlockSpec((None, CMLP_CHUNK, C_GROUPS), lambda i: (l, 0, 0)),
        ],
        out_specs=out_specs,
        out_shape=out_shape,
        compiler_params=_params("parallel"),
        name="cmlp_vn" if with_vn else "cmlp",
    )(h, h, h, h, c_ln_g3, c_ln_b3, c_ws, c_wbt)


def _split_bf16(x):
    hi = x.astype(BF16)
    lo = (x - hi.astype(F32)).astype(BF16)
    return hi, lo


def _norm1_router_kernel(x_ref, h_ref, g_ref, b_ref, rw_ref, rb_ref,
                         x1_ref, gates_ref, idx_ref, p_ref, rank_ref, counts_ref, cnt_scr, *, alpha):
    @pl.when(pl.program_id(0) == 0)
    def _():
        cnt_scr[...] = jnp.zeros_like(cnt_scr)

    x1 = _layer_norm(alpha * x_ref[...] + h_ref[...], g_ref[...], b_ref[...])
    x1_ref[...] = x1
    xh, xl = _split_bf16(x1)
    wh, wl = _split_bf16(rw_ref[...])
    dot = functools.partial(jnp.dot, preferred_element_type=F32)
    logits = dot(xh, wh) + (dot(xh, wl) + dot(xl, wh)) + rb_ref[...]
    R = logits.shape[0]
    lane = lax.broadcasted_iota(jnp.int32, (R, N_EXPERTS), 1).astype(F32)
    lane_k = lax.broadcasted_iota(jnp.int32, (R, TOP_K), 1)
    work = logits
    vals, idxs = [], []
    for _ in range(TOP_K):
        m = jnp.max(work, -1, keepdims=True)
        ix = jnp.min(jnp.where(work == m, lane, float(N_EXPERTS)), -1, keepdims=True)
        vals.append(m)
        idxs.append(ix)
        work = jnp.where(lane == ix, -jnp.inf, work)
    es = [jnp.exp(v - vals[0]) for v in vals]
    tot = es[0] + es[1] + es[2] + es[3]
    gates = jnp.zeros((R, N_EXPERTS), F32)
    idx_out = jnp.zeros((R, TOP_K), jnp.int32)
    p_out = jnp.zeros((R, TOP_K), F32)
    for k in range(TOP_K):
        pk = es[k] / tot
        gates = jnp.where(lane == idxs[k], pk, gates)
        idx_out = jnp.where(lane_k == k, idxs[k].astype(jnp.int32), idx_out)
        p_out = jnp.where(lane_k == k, pk, p_out)
    gates_ref[...] = gates
    idx_ref[...] = idx_out
    p_ref[...] = p_out
    onehot = jnp.zeros((R, N_EXPERTS), F32)
    for k in range(TOP_K):
        onehot = jnp.where(lane == idxs[k], 1.0, onehot)
    before = lax.broadcasted_iota(jnp.int32, (R, R), 0) > lax.broadcasted_iota(jnp.int32, (R, R), 1)
    prefix = jnp.dot(jnp.where(before, 1.0, 0.0).astype(BF16), onehot.astype(BF16), preferred_element_type=F32)
    base = cnt_scr[...] + prefix
    rank_out = jnp.zeros((R, TOP_K), jnp.int32)
    for k in range(TOP_K):
        rk = jnp.sum(jnp.where(lane == idxs[k], base, 0.0), -1, keepdims=True)
        rank_out = jnp.where(lane_k == k, rk.astype(jnp.int32), rank_out)
    rank_ref[...] = rank_out
    cnt_scr[...] += jnp.sum(onehot, 0, keepdims=True)
    counts_ref[...] = cnt_scr[...]


def norm1_router(x, h2, ln_g3, ln_b3, router_w, router_b3, l, alpha):
    T = x.shape[0]
    tm = _row_tile(T, 256)
    row = lambda i: (i, 0)
    vec = lambda i: (l, 0, 0)
    return pl.pallas_call(
        functools.partial(_norm1_router_kernel, alpha=alpha),
        grid=(T // tm,),
        in_specs=[
            pl.BlockSpec((tm, D_MODEL), row),
            pl.BlockSpec((tm, D_MODEL), row),
            pl.BlockSpec((None, 1, D_MODEL), vec),
            pl.BlockSpec((None, 1, D_MODEL), vec),
            pl.BlockSpec((None, D_MODEL, N_EXPERTS), vec),
            pl.BlockSpec((None, 1, N_EXPERTS), vec),
        ],
        out_specs=[
            pl.BlockSpec((tm, D_MODEL), row),
            pl.BlockSpec((tm, N_EXPERTS), row),
            pl.BlockSpec((tm, TOP_K), row),
            pl.BlockSpec((tm, TOP_K), row),
            pl.BlockSpec((tm, TOP_K), row),
            pl.BlockSpec((1, N_EXPERTS), lambda i: (0, 0)),
        ],
        out_shape=[
            jax.ShapeDtypeStruct((T, D_MODEL), F32),
            jax.ShapeDtypeStruct((T, N_EXPERTS), F32),
            jax.ShapeDtypeStruct((T, TOP_K), jnp.int32),
            jax.ShapeDtypeStruct((T, TOP_K), F32),
            jax.ShapeDtypeStruct((T, TOP_K), jnp.int32),
            jax.ShapeDtypeStruct((1, N_EXPERTS), F32),
        ],
        scratch_shapes=[pltpu.VMEM((1, N_EXPERTS), F32)],
        compiler_params=_params("arbitrary"),
        name="norm1_router",
    )(x, h2, ln_g3, ln_b3, router_w, router_b3)


def _moe_tiles(n_tokens):
    n = -(-n_tokens * TOP_K // MOE_TILE) + N_EXPERTS
    return n + n % 2


def routing_plan(idx, rank, counts):
    T = idx.shape[0]
    n_tiles = _moe_tiles(T)
    counts = counts.reshape(N_EXPERTS).astype(jnp.int32)
    padded = ((counts + MOE_TILE - 1) // MOE_TILE) * MOE_TILE
    pend = jnp.cumsum(padded)
    pstart = pend - padded
    experts = jnp.arange(N_EXPERTS, dtype=jnp.int32)
    dest = jnp.sum(jnp.where(idx[:, :, None] == experts, pstart, 0), -1) + rank
    tile_start = jnp.arange(n_tiles, dtype=jnp.int32) * MOE_TILE
    tile_e = jnp.minimum(jnp.sum((pend[None, :] <= tile_start[:, None]).astype(jnp.int32), 1), N_EXPERTS - 1)
    n_used = pend[-1] // MOE_TILE
    pad_row = jnp.where(padded > 0, pend - MOE_TILE, -1).astype(jnp.int32)
    tile = jnp.arange(n_tiles, dtype=jnp.int32)
    prev_e = jnp.concatenate([jnp.full((1,), -1, jnp.int32), tile_e[:-1]])
    first = ((tile < n_used) & (tile_e != prev_e)).astype(jnp.int32)
    run_end = jnp.sum(jnp.where(tile_e[:, None] == experts, pend // MOE_TILE, 0), -1)
    wraps = (run_end >= n_used).astype(jnp.int32)
    nxt_tile = jnp.where(wraps == 1, 0, jnp.minimum(run_end, n_tiles - 1))
    nxt_e = jnp.sum(jnp.where(nxt_tile[:, None] == tile[None, :], tile_e[None, :], 0), -1)
    plan = (tile_e, n_used.astype(jnp.int32).reshape(1), first, nxt_e.astype(jnp.int32), wraps)
    return dest, pad_row, plan


def _dma_rows(n_rows, body):
    unroll = 8
    assert n_rows % unroll == 0

    def trip(t, carry):
        for u in range(unroll):
            body(t * unroll + u)
        return carry

    lax.fori_loop(0, n_rows // unroll, trip, 0)


def _pack_bf16_pairs(x):
    c = x.shape[1] // 2
    lo = pltpu.bitcast(x[:, :c].astype(BF16).astype(F32), jnp.uint32)
    hi = pltpu.bitcast(x[:, c:].astype(BF16).astype(F32), jnp.uint32)
    return (lo >> 16) | (hi & jnp.uint32(0xFFFF0000))


def _unpack_bf16_pairs(p):
    lo = pltpu.bitcast(p << 16, F32)
    hi = pltpu.bitcast(p & jnp.uint32(0xFFFF0000), F32)
    return jnp.concatenate([lo, hi], axis=1)


def _dispatch_kernel(pad_ref, nu_ref, dest_ref, x_ref, xs_hbm, zeros_buf, packed_buf, zero_sem, row_sem):
    tm = x_ref.shape[0]
    n_tiles = xs_hbm.shape[0] // MOE_TILE

    def zero_copy(row):
        start = pl.multiple_of(row, MOE_TILE)
        return pltpu.make_async_copy(zeros_buf, xs_hbm.at[pl.ds(start, MOE_TILE), :], zero_sem)

    @pl.when(pl.program_id(0) == 0)
    def _():
        zeros_buf[...] = jnp.zeros_like(zeros_buf)
        for e in range(N_EXPERTS):
            @pl.when(pad_ref[e] >= 0)
            def _():
                zero_copy(pad_ref[e]).start()

        def start_tail(t, c):
            zero_copy(t * MOE_TILE).start()
            return c

        def wait_tail(t, c):
            zero_copy(t * MOE_TILE).wait()
            return c

        lax.fori_loop(nu_ref[0], n_tiles, start_tail, 0)
        for e in range(N_EXPERTS):
            @pl.when(pad_ref[e] >= 0)
            def _():
                zero_copy(pad_ref[e]).wait()
        lax.fori_loop(nu_ref[0], n_tiles, wait_tail, 0)

    i = pl.program_id(0)
    slot = i % 2
    packed_buf[slot] = _pack_bf16_pairs(x_ref[...])

    def start_row(r):
        for k in range(TOP_K):
            d = dest_ref[0, r * TOP_K + k]
            pltpu.make_async_copy(packed_buf.at[slot, pl.ds(r, 1), :], xs_hbm.at[pl.ds(d, 1), :],
                                  row_sem.at[slot]).start()

    _dma_rows(tm, start_row)

    def wait_rows(s):
        for k in range(TOP_K):
            pltpu.make_async_copy(packed_buf.at[s], xs_hbm.at[pl.ds(0, tm), :], row_sem.at[s]).wait()

    @pl.when(i > 0)
    def _():
        wait_rows(1 - slot)

    @pl.when(i == pl.num_programs(0) - 1)
    def _():
        wait_rows(slot)


def dispatch(x1, dest, pad_row, n_used):
    T = x1.shape[0]
    tm = _row_tile(T, 256)
    n_rows = _moe_tiles(T) * MOE_TILE
    dest3 = dest.reshape(T // tm, 1, tm * TOP_K)
    return pl.pallas_call(
        _dispatch_kernel,
        grid_spec=pltpu.PrefetchScalarGridSpec(
            num_scalar_prefetch=2,
            grid=(T // tm,),
            in_specs=[
                pl.BlockSpec((None, 1, tm * TOP_K), lambda i, *_: (i, 0, 0), memory_space=pltpu.SMEM),
                pl.BlockSpec((tm, D_MODEL), lambda i, *_: (i, 0)),
            ],
            out_specs=pl.BlockSpec(memory_space=pl.ANY),
            scratch_shapes=[
                pltpu.VMEM((MOE_TILE, D_MODEL // 2), jnp.uint32),
                pltpu.VMEM((2, tm, D_MODEL // 2), jnp.uint32),
                pltpu.SemaphoreType.DMA(()),
                pltpu.SemaphoreType.DMA((2,)),
            ],
        ),
        out_shape=jax.ShapeDtypeStruct((n_rows, D_MODEL // 2), jnp.uint32),
        compiler_params=_params("arbitrary"),
        name="dispatch",
    )(pad_row, n_used, dest3, x1)


def _switch_weights(first, very_first, own_e, own_pass, nxt_e, nxt_pass, has_next, slot_ref, copies):
    @pl.when(very_first)
    def _():
        slot_ref[0] = 1
        for c in copies(own_e, own_pass, 0):
            c.start()

    @pl.when(first)
    def _():
        slot = 1 - slot_ref[0]
        slot_ref[0] = slot
        for c in copies(own_e, own_pass, slot):
            c.wait()

        @pl.when(has_next)
        def _():
            for c in copies(nxt_e, nxt_pass, 1 - slot):
                c.start()


def _pair_step(t0, nu, first_ref, rows, switch, compute, clear):
    t1 = t0 + 1

    @pl.when(t0 < nu)
    def _():
        switch(t0)
        same_run = (t1 < nu) & (first_ref[t1] == 0)

        @pl.when(same_run)
        def _():
            compute(0, 2 * rows, t0)

        @pl.when(jnp.logical_not(same_run))
        def _():
            compute(0, rows, t0)

            @pl.when(t1 < nu)
            def _():
                switch(t1)
                compute(rows, rows, t1)

            @pl.when(t1 >= nu)
            def _():
                clear(rows, rows)

    @pl.when(t0 >= nu)
    def _():
        clear(0, 2 * rows)


def _expert_up_kernel(te_ref, nu_ref, first_ref, nxt_ref, wraps_ref, x_ref, wgu_hbm, bg_ref, bu_ref, a_ref,
                      wbuf, wsem, slot_ref, *, l, tn):
    j = pl.program_id(0)
    nj = pl.num_programs(0)

    def copies(e, jj, slot):
        cols = lambda part: pl.ds(pl.multiple_of(part * EXPERT_DIM + jj * tn, tn), tn)
        return [pltpu.make_async_copy(wgu_hbm.at[l, e, :, cols(part)], wbuf.at[slot, part], wsem.at[slot, part])
                for part in range(2)]

    def switch(t):
        _switch_weights(first_ref[t] == 1, (j == 0) & (t == 0), te_ref[t], j,
                        nxt_ref[t], j + wraps_ref[t], j + wraps_ref[t] < nj, slot_ref, copies)

    def compute(r0, n, t):
        slot = slot_ref[0]
        e = te_ref[t]
        x = _unpack_bf16_pairs(x_ref[r0:r0 + n, :]).astype(BF16)
        hg = jnp.dot(x, wbuf[slot, 0].astype(BF16), preferred_element_type=F32) + bg_ref[pl.ds(e, 1), :]
        hu = jnp.dot(x, wbuf[slot, 1].astype(BF16), preferred_element_type=F32) + bu_ref[pl.ds(e, 1), :]
        g = jnp.minimum(hg, SWIGLU_LIMIT)
        u = jnp.clip(hu, -SWIGLU_LIMIT, SWIGLU_LIMIT)
        a_ref[r0:r0 + n, :] = ((u + 1.0) * g * _sigmoid(SWIGLU_ALPHA * g)).astype(a_ref.dtype)

    def clear(r0, n):
        a_ref[r0:r0 + n, :] = jnp.zeros((n, a_ref.shape[1]), a_ref.dtype)

    _pair_step(2 * pl.program_id(1), nu_ref[0], first_ref, MOE_TILE, switch, compute, clear)


def expert_up(xs, exp_w_gu, exp_b_gu, plan, l):
    tile_e, n_used, first, nxt_e, wraps = plan
    n_rows = xs.shape[0]
    pair = 2 * MOE_TILE
    tn = 512
    nj = EXPERT_DIM // tn
    pc = lambda m, nu: jnp.minimum(m, (nu[0] + 1) // 2 - 1)
    return pl.pallas_call(
        functools.partial(_expert_up_kernel, l=l, tn=tn),
        grid_spec=pltpu.PrefetchScalarGridSpec(
            num_scalar_prefetch=5,
            grid=(nj, n_rows // pair),
            in_specs=[
                pl.BlockSpec((pair, D_MODEL // 2), lambda j, m, te, nu, *_: (pc(m, nu), 0)),
                pl.BlockSpec(memory_space=pl.ANY),
                pl.BlockSpec((None, N_EXPERTS, tn), lambda j, m, *_: (l, 0, j)),
                pl.BlockSpec((None, N_EXPERTS, tn), lambda j, m, *_: (l, 0, nj + j)),
            ],
            out_specs=pl.BlockSpec((pair, tn), lambda j, m, *_: (m, j)),
            scratch_shapes=[
                pltpu.VMEM((2, 2, D_MODEL, tn), F32),
                pltpu.SemaphoreType.DMA((2, 2)),
                pltpu.SMEM((1,), jnp.int32),
            ],
        ),
        out_shape=jax.ShapeDtypeStruct((n_rows, EXPERT_DIM), BF16),
        compiler_params=_params("arbitrary", "arbitrary"),
        name="expert_up",
    )(tile_e, n_used, first, nxt_e, wraps, xs, exp_w_gu, exp_b_gu, exp_b_gu)


def _expert_down_kernel(te_ref, nu_ref, first_ref, nxt_ref, wraps_ref, a_ref, wd_hbm, y_ref,
                        wbuf, wsem, slot_ref, *, l):
    def copies(e, _, slot):
        return [pltpu.make_async_copy(wd_hbm.at[l, e], wbuf.at[slot], wsem.at[slot])]

    def switch(t):
        _switch_weights(first_ref[t] == 1, t == 0, te_ref[t], 0, nxt_ref[t], 0, wraps_ref[t] == 0,
                        slot_ref, copies)

    def compute(r0, n, t):
        y = jnp.dot(a_ref[r0:r0 + n, :], wbuf[slot_ref[0]].astype(BF16), preferred_element_type=F32)
        y_ref[r0:r0 + n, :] = _pack_bf16_pairs(y)

    def clear(r0, n):
        y_ref[r0:r0 + n, :] = jnp.zeros((n, y_ref.shape[1]), y_ref.dtype)

    _pair_step(2 * pl.program_id(0), nu_ref[0], first_ref, MOE_TILE, switch, compute, clear)


def expert_down(a, exp_w_d, plan, l):
    tile_e, n_used, first, nxt_e, wraps = plan
    n_rows = a.shape[0]
    pair = 2 * MOE_TILE
    pc = lambda m, nu: jnp.minimum(m, (nu[0] + 1) // 2 - 1)
    return pl.pallas_call(
        functools.partial(_expert_down_kernel, l=l),
        grid_spec=pltpu.PrefetchScalarGridSpec(
            num_scalar_prefetch=5,
            grid=(n_rows // pair,),
            in_specs=[
                pl.BlockSpec((pair, EXPERT_DIM), lambda m, te, nu, *_: (pc(m, nu), 0)),
                pl.BlockSpec(memory_space=pl.ANY),
            ],
            out_specs=pl.BlockSpec((pair, D_MODEL // 2), lambda m, *_: (m, 0)),
            scratch_shapes=[
                pltpu.VMEM((2, EXPERT_DIM, D_MODEL), F32),
                pltpu.SemaphoreType.DMA((2,)),
                pltpu.SMEM((1,), jnp.int32),
            ],
        ),
        out_shape=jax.ShapeDtypeStruct((n_rows, D_MODEL // 2), jnp.uint32),
        compiler_params=_params("arbitrary"),
        name="expert_down",
    )(tile_e, n_used, first, nxt_e, wraps, a, exp_w_d)


COMBINE_LOOKAHEAD = 2


def _combine_norm2_kernel(d0_ref, d1_ref, d2_ref, x1_ref, ys_hbm, p_ref, gates_ref, bd_ref, g_ref, b_ref,
                          x2_ref, x2b_ref, rows_a, rows_b, rows_c, sems, *, alpha):
    i = pl.program_id(0)
    n = pl.num_programs(0)
    tm = x1_ref.shape[0]
    n_rows = TOP_K * tm
    bufs = (rows_a, rows_b, rows_c)

    def row_copy(d_ref, r, s):
        return pltpu.make_async_copy(ys_hbm.at[pl.ds(d_ref[0, r], 1), :], bufs[s].at[pl.ds(r, 1), :], sems.at[s])

    def wait_rows(s):
        pltpu.make_async_copy(ys_hbm.at[pl.ds(0, n_rows), :], bufs[s], sems.at[s]).wait()

    @pl.when(i == 0)
    def _():
        _dma_rows(n_rows, lambda r: row_copy(d0_ref, r, 0).start())
        _dma_rows(n_rows, lambda r: row_copy(d1_ref, r, 1).start())

    def step(cur_s):
        ahead_s = (cur_s + COMBINE_LOOKAHEAD) % 3
        cur = bufs[cur_s]
        wait_rows(cur_s)
        for r in range(n_rows):
            row_copy(d2_ref, r, ahead_s).start()
        p = p_ref[...]
        y = jnp.dot(gates_ref[...].astype(BF16), bd_ref[...].astype(BF16), preferred_element_type=F32)
        for k in range(TOP_K):
            y += p[:, k:k + 1] * _unpack_bf16_pairs(cur[k * tm:(k + 1) * tm, :])
        x2 = _layer_norm(alpha * x1_ref[...] + y, g_ref[...], b_ref[...])
        x2_ref[...] = x2
        x2b_ref[...] = x2.astype(BF16)

        @pl.when(i == n - 1)
        def _():
            wait_rows((cur_s + 1) % 3)
            wait_rows(ahead_s)

    for s in range(3):
        @pl.when(i % 3 == s)
        def _(s=s):
            step(s)


def combine_norm2(x1, ys, dest, p, gates, exp_b_d, ln_g3, ln_b3, l, alpha):
    T = x1.shape[0]
    tm = _row_tile(T, 128)
    nt = T // tm
    row = lambda i: (i, 0)
    vec = lambda i: (l, 0, 0)
    dest3 = dest.reshape(nt, tm, TOP_K).transpose(0, 2, 1).reshape(nt, 1, TOP_K * tm)
    return pl.pallas_call(
        functools.partial(_combine_norm2_kernel, alpha=alpha),
        grid=(nt,),
        in_specs=[
            pl.BlockSpec((None, 1, TOP_K * tm), lambda i: (i, 0, 0), memory_space=pltpu.SMEM),
            pl.BlockSpec((None, 1, TOP_K * tm), lambda i: (jnp.minimum(i + 1, nt - 1), 0, 0),
                         memory_space=pltpu.SMEM),
            pl.BlockSpec((None, 1, TOP_K * tm), lambda i: (jnp.minimum(i + COMBINE_LOOKAHEAD, nt - 1), 0, 0),
                         memory_space=pltpu.SMEM),
            pl.BlockSpec((tm, D_MODEL), row),
            pl.BlockSpec(memory_space=pl.ANY),
            pl.BlockSpec((tm, TOP_K), row),
            pl.BlockSpec((tm, N_EXPERTS), row),
            pl.BlockSpec((None, N_EXPERTS, D_MODEL), vec),
            pl.BlockSpec((None, 1, D_MODEL), vec),
            pl.BlockSpec((None, 1, D_MODEL), vec),
        ],
        out_specs=[pl.BlockSpec((tm, D_MODEL), row), pl.BlockSpec((tm, D_MODEL), row)],
        out_shape=[jax.ShapeDtypeStruct((T, D_MODEL), F32), jax.ShapeDtypeStruct((T, D_MODEL), BF16)],
        scratch_shapes=[
            pltpu.VMEM((TOP_K * tm, D_MODEL // 2), jnp.uint32),
            pltpu.VMEM((TOP_K * tm, D_MODEL // 2), jnp.uint32),
            pltpu.VMEM((TOP_K * tm, D_MODEL // 2), jnp.uint32),
            pltpu.SemaphoreType.DMA((3,)),
        ],
        compiler_params=_params("arbitrary"),
        name="combine_norm2",
    )(dest3, dest3, dest3, x1, ys, p, gates, exp_b_d, ln_g3, ln_b3)


def _ple_kernel(xb_ref, x_ref, p_ref, wg_ref, wp_ref, o_ref, ob_ref):
    wg = wg_ref[...].astype(BF16)
    wp = wp_ref[...].astype(BF16)
    parts = 2
    rows = x_ref.shape[0] // parts
    for part in range(parts):
        sl = pl.ds(part * rows, rows)
        gate = _sigmoid(jnp.dot(xb_ref[sl, :], wg, preferred_element_type=F32))
        pe = jnp.dot(p_ref[sl, :].astype(BF16), wp, preferred_element_type=F32)
        out = x_ref[sl, :] + gate * pe
        o_ref[sl, :] = out
        ob_ref[sl, :] = out.astype(BF16)


def ple(x2, x2b, p_l, ple_gate_w, ple_w, l):
    T = x2.shape[0]
    tm = _row_tile(T, 1088)
    tn = 512
    return pl.pallas_call(
        _ple_kernel,
        grid=(T // tm, D_MODEL // tn),
        in_specs=[
            pl.BlockSpec((tm, D_MODEL), lambda i, j: (i, 0)),
            pl.BlockSpec((tm, tn), lambda i, j: (i, j)),
            pl.BlockSpec((tm, PLE_DIM), lambda i, j: (i, 0)),
            pl.BlockSpec((None, D_MODEL, tn), lambda i, j: (l, 0, j)),
            pl.BlockSpec((None, PLE_DIM, tn), lambda i, j: (l, 0, j)),
        ],
        out_specs=[pl.BlockSpec((tm, tn), lambda i, j: (i, j)), pl.BlockSpec((tm, tn), lambda i, j: (i, j))],
        out_shape=[jax.ShapeDtypeStruct((T, D_MODEL), F32), jax.ShapeDtypeStruct((T, D_MODEL), BF16)],
        compiler_params=_params("parallel", "parallel"),
        name="ple",
    )(x2b, x2, p_l, ple_gate_w, ple_w)


def _tables(pos):
    posf = pos.astype(F32)[:, None]
    half = ROT_DIM_A // 2
    inv = jnp.float32(ROPE_THETA_A) ** (-jnp.arange(half, dtype=F32) / half)
    ang = posf * inv[None, :]
    cos, sin = jnp.cos(ang), jnp.sin(ang)
    n = pos.shape[0]
    ones = jnp.ones((n, HEAD_DIM_A - ROT_DIM_A), F32)
    zeros = jnp.zeros((n, HEAD_DIM_A - ROT_DIM_A), F32)
    z8 = jnp.zeros((n, half), F32)
    c64 = jnp.concatenate([cos, cos, ones], 1)
    lo64 = jnp.concatenate([-sin, z8, zeros], 1)
    hi64 = jnp.concatenate([z8, sin, zeros], 1)
    tab_a = jnp.concatenate([c64, c64, lo64, lo64, hi64, hi64], 1)
    hb = B_HEAD_DIM // 2
    invb = jnp.float32(RET_THETA) ** (-jnp.arange(hb, dtype=F32) / hb)
    angb = posf * invb[None, :]
    tab_b = jnp.concatenate([jnp.cos(angb), jnp.sin(angb)], 1)
    return tab_a, tab_b


def kernel(x_prompt, x_sample, cache_a_k, cache_a_v, state_b, p_prompt, p_sample, w_in, b_in, a_sinks, c_ln_g, c_ln_b, c_ws, c_wb, w_out, b_out, ln1_g, ln1_b, router_w, router_b, exp_w_gu, exp_b_gu, exp_w_d, exp_b_d, ln2_g, ln2_b, ple_w, ple_gate_w):
    depth = w_in.shape[0]
    alpha = (2 * depth) ** 0.25
    bp, S, _ = x_prompt.shape
    nb, Ls, _ = x_sample.shape
    assert bp == 1 and Ls == CHUNK and S % RET_BLOCK == 0
    win = cache_a_k.shape[2]
    n_s = nb * Ls

    log_gamma = jnp.log(1.0 - 2.0 ** (-5.0 - jnp.arange(B_HEADS, dtype=F32)))
    pos = jnp.concatenate([jnp.arange(S), jnp.tile(PAST_LEN + jnp.arange(Ls), nb)])
    tab_a, tab_b = _tables(pos)

    x = jnp.concatenate([x_prompt.reshape(S, D_MODEL), x_sample.reshape(n_s, D_MODEL)], 0)
    xb = x.astype(BF16)
    cache_k = cache_a_k.reshape(depth, nb, win, A_KV_WIDTH)
    cache_v = cache_a_v.reshape(depth, nb, win, A_KV_WIDTH)
    vec3 = lambda a: a.reshape(depth, 1, a.shape[-1])
    b_in3, b_out3 = vec3(b_in), vec3(b_out)
    c_ln_g3, c_ln_b3 = vec3(c_ln_g), vec3(c_ln_b)
    ln1_g3, ln1_b3, ln2_g3, ln2_b3 = vec3(ln1_g), vec3(ln1_b), vec3(ln2_g), vec3(ln2_b)
    router_b3 = vec3(router_b)
    c_wbt = jnp.swapaxes(c_wb, 1, 2)

    ak_p, av_p, rb_p, ak_s, av_s, rb_s, cv_s = [], [], [], [], [], [], []
    for l in range(depth):
        h = in_proj(xb, w_in, b_in3, l)
        oa_p, kr_p = attn_prompt(h, tab_a, a_sinks, l, S)
        oa_s, kr_s = attn_sample(h, tab_a, a_sinks, cache_k, cache_v, l, S, nb, Ls)
        ob_p, r_p = ret_prompt(h, tab_b, log_gamma, S)
        ob_s, r_s = ret_sample(h, tab_b, log_gamma, state_b, l, S, nb, Ls)
        (oc_p,) = cmlp(h, c_ln_g3, c_ln_b3, c_ws, c_wbt, l, 0, S // CMLP_CHUNK, CMLP_CHUNK, False)
        oc_s, vn_s = cmlp(h, c_ln_g3, c_ln_b3, c_ws, c_wbt, l, S, nb, Ls, True)
        h2 = out_proj((oa_p, ob_p, oc_p), (oa_s, ob_s, oc_s), w_out, b_out3, l)
        x1, gates, idx, p, rank, counts = norm1_router(x, h2, ln1_g3, ln1_b3, router_w, router_b3, l, alpha)

        dest, pad_row, plan = routing_plan(idx, rank, counts)
        xs = dispatch(x1, dest, pad_row, plan[1])
        a = expert_up(xs, exp_w_gu, exp_b_gu, plan, l)
        ys = expert_down(a, exp_w_d, plan, l)
        x2, x2b = combine_norm2(x1, ys, dest, p, gates, exp_b_d, ln2_g3, ln2_b3, l, alpha)

        p_l = jnp.concatenate([p_prompt[l].reshape(S, PLE_DIM), p_sample[l].reshape(n_s, PLE_DIM)], 0)
        x, xb = ple(x2, x2b, p_l, ple_gate_w, ple_w, l)

        ak_p.append(kr_p[S - win:].reshape(1, win, A_KV_HEADS, HEAD_DIM_A))
        av_p.append(h[S - win:S, OFF_AV:OFF_AV + A_KV_WIDTH].reshape(1, win, A_KV_HEADS, HEAD_DIM_A))
        rb_p.append(r_p[None])
        k_all = jnp.concatenate([cache_k[l], kr_s.reshape(nb, Ls, A_KV_WIDTH)], 1)
        v_all = jnp.concatenate([cache_v[l], h[S:, OFF_AV:OFF_AV + A_KV_WIDTH].reshape(nb, Ls, A_KV_WIDTH)], 1)
        ak_s.append(k_all[:, -win:].reshape(nb, win, A_KV_HEADS, HEAD_DIM_A))
        av_s.append(v_all[:, -win:].reshape(nb, win, A_KV_HEADS, HEAD_DIM_A))
        rb_s.append(r_s)
        cv_s.append(vn_s.reshape(nb, Ls, C_WIDTH))

    return (x[:S].reshape(1, S, D_MODEL), x[S:].reshape(nb, Ls, D_MODEL),
            jnp.stack(ak_p), jnp.stack(av_p), jnp.stack(rb_p),
            jnp.stack(ak_s), jnp.stack(av_s), jnp.stack(rb_s), jnp.stack(cv_s))
```
